```python
import functools
import jax, jax.numpy as jnp
from jax import lax
import numpy as np

D_MODEL = 2048
BATCH = 8
SEQ = 2048
DEPTH = 2
DEC_BATCH = 128
DEC_SEQ = 4
PAST_LEN = 16384
PAGE_SIZE = 128

N_MIXERS = 2
N_META = 16
BLOCK = 128
EPS = 1e-6
NEG_INF = -1e30

MLA_HEADS = 16
MLA_Q_LORA = 512
MLA_KV_LORA = 512
MLA_NOPE = 128
MLA_ROPE = 64
MLA_V = 128
MLA_BRANCH = MLA_HEADS * MLA_V
MLA_IN = MLA_Q_LORA + MLA_KV_LORA + MLA_ROPE + MLA_BRANCH
MLA_SCALE = (MLA_NOPE + MLA_ROPE) ** -0.5
ROPE_THETA = 10000.0

SB_HEADS = 16
SB_KV_HEADS = 2
SB_GROUP = SB_HEADS // SB_KV_HEADS
SB_HEAD_DIM = 128
SB_BRANCH = SB_HEADS * SB_HEAD_DIM
SB_KVW = SB_KV_HEADS * SB_HEAD_DIM
SB_IN = SB_BRANCH + 2 * SB_KVW + SB_BRANCH
SB_SCALE = SB_HEAD_DIM ** -0.5

N_MLA_LAYERS = (DEPTH + N_MIXERS - 1) // N_MIXERS
N_SB_LAYERS = DEPTH // N_MIXERS

kernel_name = 'hybrid_mla_stickbreak_decoder'


def rmsnorm(x, g):
    x32 = x.astype(jnp.float32)
    y = x32 * lax.rsqrt(jnp.mean(x32 * x32, axis=-1, keepdims=True) + EPS)
    return (y * g.astype(jnp.float32)).astype(x.dtype)


def rope(x, pos):
    half = x.shape[-1] // 2
    inv = ROPE_THETA ** (-jnp.arange(half, dtype=jnp.float32) / half)
    ang = pos.astype(jnp.float32)[:, None] * inv[None, :]
    shape = (1, pos.shape[0]) + (1,) * (x.ndim - 3) + (half,)
    cos = jnp.cos(ang).reshape(shape)
    sin = jnp.sin(ang).reshape(shape)
    x1 = x[..., :half].astype(jnp.float32)
    x2 = x[..., half:].astype(jnp.float32)
    return jnp.concatenate([x1 * cos - x2 * sin, x2 * cos + x1 * sin], axis=-1).astype(x.dtype)


def gated_out(o, gate, w_out, dtype):
    o = o.reshape(o.shape[:2] + (-1,)) * jax.nn.silu(gate.astype(jnp.float32))
    return o.astype(dtype) @ w_out


def mla_project(h, pos, g_norm, w_in, g_q, g_kv, w_uq, w_uk):
    u = rmsnorm(h, g_norm) @ w_in
    cq, ckv, kr, gate = jnp.split(
        u, [MLA_Q_LORA, MLA_Q_LORA + MLA_KV_LORA, MLA_Q_LORA + MLA_KV_LORA + MLA_ROPE], axis=-1)
    q = jnp.einsum('blc,chn->blhn', rmsnorm(cq, g_q), w_uq)
    q_lat = jnp.einsum('blhn,rhn->blhr', q[..., :MLA_NOPE], w_uk)
    q_rope = rope(q[..., MLA_NOPE:], pos)
    c = rmsnorm(ckv, g_kv)
    k_rope = rope(kr, pos)
    return (q_lat, q_rope), (c, k_rope), gate


def mla_core(qs, ks, qpos, kpos, w_uv):
    q_lat, q_rope = qs
    c, k_rope = ks
    s = (jnp.einsum('qhr,kr->hqk', q_lat, c, preferred_element_type=jnp.float32)
         + jnp.einsum('qhp,kp->hqk', q_rope, k_rope, preferred_element_type=jnp.float32)) * MLA_SCALE
    mask = (kpos[None, :] <= qpos[:, None]) & (kpos[None, :] >= 0)
    p = jax.nn.softmax(jnp.where(mask, s, NEG_INF), axis=-1)
    o_lat = jnp.einsum('hqk,kr->qhr', p, c.astype(jnp.float32))
    return jnp.einsum('qhr,rhv->qhv', o_lat, w_uv.astype(jnp.float32))


def sb_project(h, g_norm, w_in):
    u = rmsnorm(h, g_norm) @ w_in
    q, k, v, gate = jnp.split(u, [SB_BRANCH, SB_BRANCH + SB_KVW, SB_BRANCH + 2 * SB_KVW], axis=-1)
    b, l = h.shape[:2]
    q = q.reshape(b, l, SB_KV_HEADS, SB_GROUP, SB_HEAD_DIM)
    k = k.reshape(b, l, SB_KV_HEADS, SB_HEAD_DIM)
    v = v.reshape(b, l, SB_KV_HEADS, SB_HEAD_DIM)
    return (q,), (k, v), gate


def sb_core(qs, ks, qpos, kpos):
    (q,) = qs
    k, v = ks
    z = jnp.einsum('qghd,kgd->ghqk', q, k, preferred_element_type=jnp.float32) * SB_SCALE
    mask = (kpos[None, :] < qpos[:, None]) & (kpos[None, :] >= 0)
    log_keep = jnp.where(mask, jax.nn.log_sigmoid(-z), 0.0)
    log_between = lax.cumsum(log_keep, axis=3, reverse=True) - log_keep
    a = jnp.where(mask, jnp.exp(jax.nn.log_sigmoid(z) + log_between), 0.0)
    o = jnp.einsum('ghqk,kgd->qghd', a, v.astype(jnp.float32))
    return o.reshape(o.shape[0], SB_HEADS, SB_HEAD_DIM)


def prompt_attend(core, qs, ks):
    b, l = qs[0].shape[:2]
    pad = (-l) % BLOCK
    lp = l + pad
    nb = lp // BLOCK

    def padf(a):
        return jnp.pad(a, [(0, 0), (pad, 0)] + [(0, 0)] * (a.ndim - 2))

    qs_p = tuple(padf(a) for a in qs)
    ks_p = tuple(padf(a) for a in ks)
    kpos = jnp.arange(lp, dtype=jnp.int32) - pad
    q_blocks = tuple(jnp.moveaxis(a.reshape((b, nb, BLOCK) + a.shape[2:]), 1, 0) for a in qs_p)
    qpos_blocks = kpos.reshape(nb, BLOCK)

    def one_block(args):
        qb, qpos = args
        return jax.vmap(lambda qq, kk: core(qq, kk, qpos, kpos))(qb, ks_p)

    out = lax.map(one_block, (q_blocks, qpos_blocks))
    out = jnp.moveaxis(out, 0, 1)
    out = out.reshape((b, lp) + out.shape[3:])
    return out[:, pad:]


def paged_sample_attend(core, qs, ks_new, pools, layer, page_table):
    past = page_table.shape[1] * PAGE_SIZE
    t = qs[0].shape[1]
    qpos = past + jnp.arange(t, dtype=jnp.int32)
    kpos = jnp.arange(past + t, dtype=jnp.int32)

    def one_seq(args):
        qq, kn, pt = args
        kk = tuple(jnp.concatenate([pool[layer, pt].reshape((past,) + pool.shape[3:]), n.astype(pool.dtype)], axis=0)
                   for pool, n in zip(pools, kn))
        return core(qq, kk, qpos, kpos)

    return lax.map(one_seq, (qs, ks_new, page_table))


def setup_inputs(seed: int = 0) -> dict:
    key = jax.random.key(seed)
    ks = jax.random.split(key, 24)
    f32 = jnp.float32
    n_pages = PAST_LEN // PAGE_SIZE
    n_used = DEC_BATCH * n_pages
    n_pool = n_used + n_used // 4 + 1

    def nrm(k, shape, scale=1.0):
        return jax.random.normal(k, shape, f32) * scale

    def gain(k, shape):
        return 1.0 + 0.01 * jax.random.normal(k, shape, f32)

    page_table = jax.random.permutation(ks[6], n_pool)[:n_used].reshape(DEC_BATCH, n_pages).astype(jnp.int32)
    return {
        'x_prompt': nrm(ks[0], (BATCH, SEQ, D_MODEL)),
        'x_sample': nrm(ks[1], (DEC_BATCH, DEC_SEQ, D_MODEL)),
        'cache_mla_latent': nrm(ks[2], (N_MLA_LAYERS, n_pool, PAGE_SIZE, MLA_KV_LORA)),
        'cache_mla_krope': nrm(ks[3], (N_MLA_LAYERS, n_pool, PAGE_SIZE, MLA_ROPE)),
        'cache_sb_k': nrm(ks[4], (N_SB_LAYERS, n_pool, PAGE_SIZE, SB_KV_HEADS, SB_HEAD_DIM)),
        'cache_sb_v': nrm(ks[5], (N_SB_LAYERS, n_pool, PAGE_SIZE, SB_KV_HEADS, SB_HEAD_DIM)),
        'page_table': page_table,
        'meta_tokens': nrm(ks[7], (N_META, D_MODEL)),
        'mla_norm': gain(ks[8], (N_MLA_LAYERS, D_MODEL)),
        'mla_w_in': nrm(ks[9], (N_MLA_LAYERS, D_MODEL, MLA_IN), D_MODEL ** -0.5),
        'mla_q_norm': gain(ks[10], (N_MLA_LAYERS, MLA_Q_LORA)),
        'mla_kv_norm': gain(ks[11], (N_MLA_LAYERS, MLA_KV_LORA)),
        'mla_w_uq': nrm(ks[12], (N_MLA_LAYERS, MLA_Q_LORA, MLA_HEADS, MLA_NOPE + MLA_ROPE), MLA_Q_LORA ** -0.5),
        'mla_w_uk': nrm(ks[13], (N_MLA_LAYERS, MLA_KV_LORA, MLA_HEADS, MLA_NOPE), MLA_KV_LORA ** -0.5),
        'mla_w_uv': nrm(ks[14], (N_MLA_LAYERS, MLA_KV_LORA, MLA_HEADS, MLA_V), MLA_KV_LORA ** -0.5),
        'mla_w_out': nrm(ks[15], (N_MLA_LAYERS, MLA_BRANCH, D_MODEL), MLA_BRANCH ** -0.5),
        'sb_norm': gain(ks[16], (N_SB_LAYERS, D_MODEL)),
        'sb_w_in': nrm(ks[17], (N_SB_LAYERS, D_MODEL, SB_IN), D_MODEL ** -0.5),
        'sb_w_out': nrm(ks[18], (N_SB_LAYERS, SB_BRANCH, D_MODEL), SB_BRANCH ** -0.5),
        'final_norm': gain(ks[19], (D_MODEL,)),
    }


def reference(x_prompt, x_sample, cache_mla_latent, cache_mla_krope, cache_sb_k, cache_sb_v, page_table,
              meta_tokens, mla_norm, mla_w_in, mla_q_norm, mla_kv_norm, mla_w_uq, mla_w_uk, mla_w_uv, mla_w_out,
              sb_norm, sb_w_in, sb_w_out, final_norm):
    b = x_prompt.shape[0]
    meta = jnp.broadcast_to(meta_tokens.astype(x_prompt.dtype)[None], (b, N_META, meta_tokens.shape[-1]))
    hp = jnp.concatenate([meta, x_prompt], axis=1)
    pos_p = jnp.arange(hp.shape[1], dtype=jnp.int32)
    hs = x_sample
    pos_s = page_table.shape[1] * PAGE_SIZE + jnp.arange(hs.shape[1], dtype=jnp.int32)

    lat_p, kr_p, sbk_p, sbv_p = [], [], [], []
    lat_s, kr_s, sbk_s, sbv_s = [], [], [], []
    for i in range(DEPTH):
        j = i // N_MIXERS
        if i % N_MIXERS == 0:
            core = functools.partial(mla_core, w_uv=mla_w_uv[j])
            qs, kv, gate = mla_project(hp, pos_p, mla_norm[j], mla_w_in[j], mla_q_norm[j], mla_kv_norm[j],
                                       mla_w_uq[j], mla_w_uk[j])
            hp = hp + gated_out(prompt_attend(core, qs, kv), gate, mla_w_out[j], hp.dtype)
            lat_p.append(kv[0])
            kr_p.append(kv[1])
            qs, kv, gate = mla_project(hs, pos_s, mla_norm[j], mla_w_in[j], mla_q_norm[j], mla_kv_norm[j],
                                       mla_w_uq[j], mla_w_uk[j])
            o = paged_sample_attend(core, qs, kv, (cache_mla_latent, cache_mla_krope), j, page_table)
            hs = hs + gated_out(o, gate, mla_w_out[j], hs.dtype)
            lat_s.append(kv[0])
            kr_s.append(kv[1])
        else:
            qs, kv, gate = sb_project(hp, sb_norm[j], sb_w_in[j])
            hp = hp + gated_out(prompt_attend(sb_core, qs, kv), gate, sb_w_out[j], hp.dtype)
            sbk_p.append(kv[0])
            sbv_p.append(kv[1])
            qs, kv, gate = sb_project(hs, sb_norm[j], sb_w_in[j])
            o = paged_sample_attend(sb_core, qs, kv, (cache_sb_k, cache_sb_v), j, page_table)
            hs = hs + gated_out(o, gate, sb_w_out[j], hs.dtype)
            sbk_s.append(kv[0])
            sbv_s.append(kv[1])

    y_prompt = rmsnorm(hp, final_norm)[:, N_META:]
    y_sample = rmsnorm(hs, final_norm)
    return (y_prompt, y_sample,
            jnp.stack(lat_p), jnp.stack(kr_p), jnp.stack(sbk_p), jnp.stack(sbv_p),
            jnp.stack(lat_s), jnp.stack(kr_s), jnp.stack(sbk_s), jnp.stack(sbv_s))
```

```python
import functools

import jax
import jax.numpy as jnp
from jax import lax
from jax.experimental import pallas as pl
from jax.experimental.pallas import tpu as pltpu

EPS = 1e-6
NEG_INF = -1e30
ROPE_THETA = 10000.0
LANES = 128
ROW_ALIGN = 128
VMEM_LIMIT = 56 * 1024 * 1024
PAGES_PER_STEP = 16
SB_KEY_BLOCK = 256

F32 = jnp.float32
BF16 = jnp.bfloat16


def _cparams(*sem):
    return pltpu.CompilerParams(dimension_semantics=sem, vmem_limit_bytes=VMEM_LIMIT)


def _rms(x, g):
    return x * lax.rsqrt(jnp.mean(x * x, axis=-1, keepdims=True) + EPS) * g


def _dot(a, b):
    return jnp.dot(a, b, preferred_element_type=F32)


def _dot_nt(a, b):
    return lax.dot_general(a, b, (((1,), (1,)), ((), ())), preferred_element_type=F32)


def _silu(x):
    return x * (1.0 / (1.0 + jnp.exp(-x)))


def _softplus(z):
    return jnp.maximum(z, 0.0) + jnp.log1p(jnp.exp(-jnp.abs(z)))


def _const_spec(shape):
    nd = len(shape)
    return pl.BlockSpec(shape, lambda *_: (0,) * nd, pipeline_mode=pl.Buffered(1))


def _mla_in_kernel(h_ref, g_ref, w_ref, gq_ref, gkv_ref, cs_ref, sn_ref,
                   cqn_ref, c_ref, kr_ref, krp_ref, gate_ref, *, lq, lkv, nbr, rope):
    xn = _rms(h_ref[...], g_ref[...]).astype(BF16)
    u = _dot(xn, w_ref[...])
    cqn_ref[...] = _rms(u[:, :lq], gq_ref[...]).astype(BF16)
    c_ref[...] = _rms(u[:, lq:lq + lkv], gkv_ref[...])
    o = lq + lkv
    gate_ref[...] = u[:, o:o + nbr]
    o += nbr
    kr = u[:, o:o + LANES] * cs_ref[...] + u[:, o + LANES:o + 2 * LANES] * sn_ref[...]
    kr_ref[...] = kr[:, :rope]
    krp_ref[...] = kr.astype(BF16)


def _mla_up_kernel(cqn_ref, c_ref, cs_ref, sn_ref, wqn_ref, wqa_ref, wqb_ref, wuk_ref, wuv_ref,
                   qn_ref, qr_ref, kn_ref, v_ref, *, heads):
    cqn = cqn_ref[...]
    qn_ref[...] = _dot(cqn, wqn_ref[...]).astype(BF16)
    qa = _dot(cqn, wqa_ref[...])
    qb = _dot(cqn, wqb_ref[...])
    cs = cs_ref[...]
    sn = sn_ref[...]
    for hh in range(heads):
        sl = slice(hh * LANES, (hh + 1) * LANES)
        qr_ref[:, sl] = (qa[:, sl] * cs + qb[:, sl] * sn).astype(BF16)
    c16 = c_ref[...].astype(BF16)
    kn_ref[...] = _dot(c16, wuk_ref[...]).astype(BF16)
    v_ref[...] = _dot(c16, wuv_ref[...]).astype(BF16)


def _mla_prompt_attn_kernel(qn_ref, qr_ref, kn_ref, kr_ref, v_ref, gate_ref, o_ref, *, chunks, pad, scale):
    for r0, r1 in chunks:
        nq = r1 - r0
        q = jnp.concatenate([qn_ref[r0:r1, :], qr_ref[r0:r1, :]], axis=-1)
        k = jnp.concatenate([kn_ref[0:r1, :], kr_ref[0:r1, :]], axis=-1)
        s = _dot_nt(q, k) * scale
        row = r0 + lax.broadcasted_iota(jnp.int32, (nq, r1), 0)
        col = lax.broadcasted_iota(jnp.int32, (nq, r1), 1)
        s = jnp.where((col <= row) & (col >= pad), s, NEG_INF)
        m = jnp.max(s, axis=-1, keepdims=True)
        p = jnp.exp(s - m)
        l = jnp.sum(p, axis=-1, keepdims=True)
        o = _dot(p.astype(BF16), v_ref[0:r1, :]) * (1.0 / l)
        o_ref[r0:r1, :] = (o * _silu(gate_ref[r0:r1, :])).astype(BF16)


def _mla_qlat_kernel(qn_ref, wukt_ref, o_ref):
    o_ref[...] = _dot(qn_ref[...], wukt_ref[...]).astype(BF16)


def _mla_sample_attn_kernel(pt_ref, ql_ref, qr_ref, cnew_ref, krnew_ref, *refs, npp, t_new, heads, scale):
    lat_refs = refs[:npp]
    kr_refs = refs[npp:2 * npp]
    o_ref = refs[2 * npp]
    m_ref, l_ref, acc_ref = refs[2 * npp + 1:]
    j = pl.program_id(1)
    ql = ql_ref[...]
    qr = qr_ref[...]
    nrow = ql.shape[0]

    def update(c16, kr16, mask):
        s = (_dot_nt(ql, c16) + _dot_nt(qr, kr16)) * scale
        if mask is not None:
            s = jnp.where(mask, s, NEG_INF)
        m_old = m_ref[...]
        m_new = jnp.maximum(m_old, jnp.max(s, axis=-1, keepdims=True))
        alpha = jnp.exp(m_old - m_new)
        p = jnp.exp(s - m_new)
        l_ref[...] = l_ref[...] * alpha + jnp.sum(p, axis=-1, keepdims=True)
        acc_ref[...] = acc_ref[...] * alpha + _dot(p.astype(BF16), c16)
        m_ref[...] = m_new

    @pl.when(j == 0)
    def _():
        m_ref[...] = jnp.full(m_ref.shape, NEG_INF, F32)
        l_ref[...] = jnp.zeros(l_ref.shape, F32)
        acc_ref[...] = jnp.zeros(acc_ref.shape, F32)
        nn = cnew_ref.shape[0]
        c16 = jnp.concatenate([cnew_ref[...], jnp.zeros((LANES - nn, cnew_ref.shape[1]), F32)], axis=0).astype(BF16)
        kr16 = jnp.concatenate([krnew_ref[...], jnp.zeros((LANES - nn, krnew_ref.shape[1]), F32)], axis=0).astype(BF16)
        tq = _token_of_row(lax.broadcasted_iota(jnp.int32, (nrow, LANES), 0), heads, t_new)
        col = lax.broadcasted_iota(jnp.int32, (nrow, LANES), 1)
        update(c16, kr16, col <= tq)

    c16 = jnp.concatenate([r[...].astype(BF16) for r in lat_refs], axis=0)
    kr16 = jnp.concatenate([r[...].astype(BF16) for r in kr_refs], axis=0)
    update(c16, kr16, None)

    @pl.when(j == pl.num_programs(1) - 1)
    def _():
        o_ref[...] = (acc_ref[...] * (1.0 / l_ref[...])).astype(BF16)


def _mla_sample_out_kernel(ol_ref, wuv_ref, gate_ref, o_ref):
    o = _dot(ol_ref[...], wuv_ref[...])
    o_ref[...] = (o * _silu(gate_ref[...])).astype(BF16)


def _outproj_kernel(h_ref, ogp_ref, ogs_ref, w_ref, o_ref, *, n_prompt_blocks):
    i = pl.program_id(0)

    @pl.when(i < n_prompt_blocks)
    def _():
        o_ref[...] = h_ref[...] + _dot(ogp_ref[...], w_ref[...])

    @pl.when(i >= n_prompt_blocks)
    def _():
        o_ref[...] = h_ref[...] + _dot(ogs_ref[...], w_ref[...])


def _sb_in_kernel(h_ref, g_ref, w_ref, q_ref, k_ref, v_ref, gate_ref, *, nbr, kvw):
    xn = _rms(h_ref[...], g_ref[...]).astype(BF16)
    u = _dot(xn, w_ref[...])
    q_ref[...] = u[:, :nbr].astype(BF16)
    k_ref[...] = u[:, nbr:nbr + kvw]
    v_ref[...] = u[:, nbr + kvw:nbr + 2 * kvw]
    gate_ref[...] = u[:, nbr + 2 * kvw:]


def _outproj_final_kernel(h_ref, og_ref, w_ref, g_ref, y_ref):
    h = h_ref[...] + _dot(og_ref[...], w_ref[...])
    y_ref[...] = _rms(h, g_ref[...])


def _suffix_sums(lk, tri):
    hi = lk.astype(BF16)
    lo = (lk - hi.astype(F32)).astype(BF16)
    return _dot(hi, tri) + _dot(lo, tri)


def _sb_prompt_attn_kernel(q_ref, k_ref, v_ref, gate_ref, tri_ref, o_ref, *, chunks, pad, scale, kb):
    for r0, r1 in chunks:
        nq = r1 - r0
        q = q_ref[r0:r1, :]
        carry = jnp.zeros((nq, 1), F32)
        acc = jnp.zeros((nq, q.shape[1]), F32)
        nblk = -(-r1 // kb)
        for jb in reversed(range(nblk)):
            c0 = jb * kb
            c1 = min(c0 + kb, r1)
            w = c1 - c0
            tri = tri_ref[0:w, 0:w]
            z = _dot_nt(q, k_ref[c0:c1, :].astype(BF16)) * scale
            lk = -_softplus(z)
            need_mask = not (c1 <= r0 and c0 >= pad)
            if need_mask:
                row = r0 + lax.broadcasted_iota(jnp.int32, (nq, w), 0)
                col = c0 + lax.broadcasted_iota(jnp.int32, (nq, w), 1)
                mask = (col < row) & (col >= pad)
                lk = jnp.where(mask, lk, 0.0)
            ssum = _suffix_sums(lk, tri)
            a = jnp.exp(z + ssum + carry)
            if need_mask:
                a = jnp.where(mask, a, 0.0)
            acc = acc + _dot(a.astype(BF16), v_ref[c0:c1, :].astype(BF16))
            carry = carry + ssum[:, 0:1]
        o_ref[r0:r1, :] = (acc * _silu(gate_ref[r0:r1, :])).astype(BF16)


def _token_of_row(row, rows_per_token, n_tokens):
    t = jnp.zeros(row.shape, jnp.int32)
    for i in range(1, n_tokens):
        t = t + (row >= i * rows_per_token).astype(jnp.int32)
    return t


def _sb_sample_attn_kernel(pt_ref, q_ref, knew_ref, vnew_ref, tri_ref, *refs, npp, groups, hd, scale, t_new):
    k_refs = refs[:npp]
    v_refs = refs[npp:2 * npp]
    o_ref = refs[2 * npp]
    carry_ref, acc_ref = refs[2 * npp + 1:]
    j = pl.program_id(1)
    tri = tri_ref[...]
    rpg = q_ref.shape[1]
    nrow = groups * rpg
    qs = [q_ref[g] for g in range(groups)]

    def scores(k16):
        return jnp.concatenate([_dot_nt(qs[g], k16[:, g * hd:(g + 1) * hd]) for g in range(groups)], axis=0)

    def accumulate(a, v16):
        a16 = a.astype(BF16)
        upd = jnp.concatenate([_dot(a16[g * rpg:(g + 1) * rpg, :], v16[:, g * hd:(g + 1) * hd])
                               for g in range(groups)], axis=0)
        acc_ref[...] += upd

    @pl.when(j == 0)
    def _():
        acc_ref[...] = jnp.zeros(acc_ref.shape, F32)
        nn = knew_ref.shape[0]
        zpad = jnp.zeros((LANES - nn, knew_ref.shape[1]), F32)
        k16 = jnp.concatenate([knew_ref[...], zpad], axis=0).astype(BF16)
        v16 = jnp.concatenate([vnew_ref[...], zpad], axis=0).astype(BF16)
        z = scores(k16) * scale
        tq1 = _token_of_row(lax.broadcasted_iota(jnp.int32, (rpg, LANES), 0), rpg // t_new, t_new)
        tq = jnp.concatenate([tq1] * groups, axis=0)
        col = lax.broadcasted_iota(jnp.int32, (nrow, LANES), 1)
        mask = col < tq
        lk = jnp.where(mask, -_softplus(z), 0.0)
        ssum = _suffix_sums(lk, tri)
        a = jnp.where(mask, jnp.exp(z + ssum), 0.0)
        accumulate(a, v16)
        carry_ref[...] = jnp.broadcast_to(ssum[:, 0:1], carry_ref.shape)

    k16s = [r[...].astype(BF16) for r in k_refs]
    z = jnp.concatenate([scores(k16) for k16 in k16s], axis=0) * scale
    ssum = _suffix_sums(-_softplus(z), tri)
    carry = carry_ref[...]
    for kk in range(npp):
        sl = slice(kk * nrow, (kk + 1) * nrow)
        sk = ssum[sl, :]
        a = jnp.exp(z[sl, :] + sk + carry)
        accumulate(a, v_refs[kk][...].astype(BF16))
        carry = carry + sk[:, 0:1]
    carry_ref[...] = carry

    @pl.when(j == pl.num_programs(1) - 1)
    def _():
        o_ref[...] = acc_ref[...]


def _gate_kernel(o_ref, gate_ref, og_ref):
    og_ref[...] = (o_ref[...] * _silu(gate_ref[...])).astype(BF16)


def _query_chunks(lp, target=512):
    n = max(lp // target, 1)
    bounds = [i * target for i in range(n)] + [lp]
    return tuple((bounds[i], bounds[i + 1]) for i in range(n))


def kernel(x_prompt, x_sample, cache_mla_latent, cache_mla_krope, cache_sb_k, cache_sb_v, page_table,
           meta_tokens, mla_norm, mla_w_in, mla_q_norm, mla_kv_norm, mla_w_uq, mla_w_uk, mla_w_uv, mla_w_out,
           sb_norm, sb_w_in, sb_w_out, final_norm):
    B, S, D = x_prompt.shape
    NB, T = x_sample.shape[:2]
    n_meta = meta_tokens.shape[0]
    n_pages = page_table.shape[1]
    page = cache_mla_latent.shape[2]
    lq = mla_q_norm.shape[-1]
    lkv = mla_kv_norm.shape[-1]
    H = mla_w_uq.shape[2]
    nope = mla_w_uk.shape[-1]
    rope = mla_w_uq.shape[-1] - nope
    vd = mla_w_uv.shape[-1]
    nbr = H * vd
    half = rope // 2
    G = cache_sb_k.shape[3]
    hd = cache_sb_k.shape[4]
    kvw = G * hd
    sb_nbr = sb_w_out.shape[1]
    SH = sb_nbr // hd
    HG = SH // G
    assert nope == LANES and vd == LANES and hd == LANES and page == LANES and 2 * rope == LANES
    assert mla_norm.shape[0] == 1 and sb_norm.shape[0] == 1, "one layer of each mixer"
    assert n_pages % PAGES_PER_STEP == 0 and T <= 8

    L = n_meta + S
    pad = (-L) % ROW_ALIGN
    LP = L + pad
    RP = B * LP
    RS = NB * T
    R = RP + RS
    tm = 256
    assert RP % tm == 0 and RS % tm == 0 and S % ROW_ALIGN == 0
    nblk = R // tm
    past = n_pages * page

    hp = jnp.concatenate([jnp.zeros((B, pad, D), F32),
                          jnp.broadcast_to(meta_tokens.astype(F32)[None], (B, n_meta, D)), x_prompt], axis=1)
    h0 = jnp.concatenate([hp.reshape(RP, D), x_sample.reshape(RS, D)], axis=0)

    pos = jnp.concatenate([jnp.tile(jnp.arange(LP, dtype=jnp.int32) - pad, B),
                           jnp.tile(past + jnp.arange(T, dtype=jnp.int32), NB)])
    inv = ROPE_THETA ** (-jnp.arange(half, dtype=F32) / half)
    ang = pos.astype(F32)[:, None] * inv[None, :]
    zpadl = jnp.zeros((R, LANES - rope), F32)
    cs = jnp.concatenate([jnp.cos(ang), jnp.cos(ang), zpadl], axis=1)
    sn = jnp.concatenate([jnp.sin(ang), jnp.sin(ang), zpadl], axis=1)

    def rot_cols(w):
        return jnp.concatenate([-w[..., half:], w[..., :half]], axis=-1)

    def lane_pad(w):
        return jnp.concatenate([w, jnp.zeros(w.shape[:-1] + (LANES - w.shape[-1],), w.dtype)], axis=-1)

    w_in = mla_w_in[0]
    o_kr = lq + lkv
    w_kr = w_in[:, o_kr:o_kr + rope]
    w_in_ext = jnp.concatenate([w_in[:, :o_kr], w_in[:, o_kr + rope:], lane_pad(w_kr), lane_pad(rot_cols(w_kr))],
                               axis=1).astype(BF16)
    n_in = w_in_ext.shape[1]
    w_uq = mla_w_uq[0]
    w_qn = w_uq[:, :, :nope].reshape(lq, H * nope).astype(BF16)
    w_qa = lane_pad(w_uq[:, :, nope:]).reshape(lq, H * LANES).astype(BF16)
    w_qb = lane_pad(rot_cols(w_uq[:, :, nope:])).reshape(lq, H * LANES).astype(BF16)
    w_uk = mla_w_uk[0].reshape(lkv, H * nope).astype(BF16)
    w_uv = mla_w_uv[0].reshape(lkv, H * vd).astype(BF16)
    w_ukt = jnp.transpose(mla_w_uk[0], (1, 2, 0)).astype(BF16)
    w_out0 = mla_w_out[0].astype(BF16)
    w_sb_in = sb_w_in[0].astype(BF16)
    w_out1 = sb_w_out[0].astype(BF16)
    g0 = mla_norm[0].reshape(1, D)
    gq = mla_q_norm[0].reshape(1, lq)
    gkv = mla_kv_norm[0].reshape(1, lkv)
    g1 = sb_norm[0].reshape(1, D)
    gf = final_norm.reshape(1, D)

    row_spec = lambda w: pl.BlockSpec((tm, w), lambda i: (i, 0))

    cqn, c_all, kr_all, krp_all, gate0 = pl.pallas_call(
        functools.partial(_mla_in_kernel, lq=lq, lkv=lkv, nbr=nbr, rope=rope),
        grid=(nblk,),
        in_specs=[row_spec(D), _const_spec((1, D)), _const_spec((D, n_in)), _const_spec((1, lq)),
                  _const_spec((1, lkv)), row_spec(LANES), row_spec(LANES)],
        out_specs=[row_spec(lq), row_spec(lkv), row_spec(rope), row_spec(LANES), row_spec(nbr)],
        out_shape=[jax.ShapeDtypeStruct((R, lq), BF16), jax.ShapeDtypeStruct((R, lkv), F32),
                   jax.ShapeDtypeStruct((R, rope), F32), jax.ShapeDtypeStruct((R, LANES), BF16),
                   jax.ShapeDtypeStruct((R, nbr), F32)],
        compiler_params=_cparams("parallel"), name="mla_in",
    )(h0, g0, w_in_ext, gq, gkv, cs, sn)

    qn_all, qr_all, kn_all, v_all = pl.pallas_call(
        functools.partial(_mla_up_kernel, heads=H),
        grid=(nblk,),
        in_specs=[row_spec(lq), row_spec(lkv), row_spec(LANES), row_spec(LANES),
                  _const_spec((lq, H * nope)), _const_spec((lq, H * LANES)), _const_spec((lq, H * LANES)),
                  _const_spec((lkv, H * nope)), _const_spec((lkv, H * vd))],
        out_specs=[row_spec(H * nope), row_spec(H * LANES), row_spec(H * nope), row_spec(H * vd)],
        out_shape=[jax.ShapeDtypeStruct((R, H * nope), BF16), jax.ShapeDtypeStruct((R, H * LANES), BF16),
                   jax.ShapeDtypeStruct((R, H * nope), BF16), jax.ShapeDtypeStruct((R, H * vd), BF16)],
        compiler_params=_cparams("parallel"), name="mla_up",
    )(cqn, c_all, cs, sn, w_qn, w_qa, w_qb, w_uk, w_uv)

    chunks = _query_chunks(LP)
    mla_scale = float(nope + rope) ** -0.5
    head_spec = pl.BlockSpec((LP, LANES), lambda b, h: (b, h))
    ogp0 = pl.pallas_call(
        functools.partial(_mla_prompt_attn_kernel, chunks=chunks, pad=pad, scale=mla_scale),
        grid=(B, H),
        in_specs=[head_spec, head_spec, head_spec, pl.BlockSpec((LP, LANES), lambda b, h: (b, 0)),
                  head_spec, head_spec],
        out_specs=head_spec,
        out_shape=jax.ShapeDtypeStruct((RP, nbr), BF16),
        compiler_params=_cparams("parallel", "parallel"), name="mla_prompt_attn",
    )(qn_all, qr_all, kn_all, krp_all, v_all, gate0)

    sblk = RP // RS
    assert RP % RS == 0
    qlat = pl.pallas_call(
        _mla_qlat_kernel,
        grid=(H,),
        in_specs=[pl.BlockSpec((RS, nope), lambda h: (sblk, h)),
                  pl.BlockSpec((None, nope, lkv), lambda h: (h, 0, 0))],
        out_specs=pl.BlockSpec((None, RS, lkv), lambda h: (h, 0, 0)),
        out_shape=jax.ShapeDtypeStruct((H, RS, lkv), BF16),
        compiler_params=_cparams("parallel"), name="mla_qlat",
    )(qn_all, w_ukt)
    qlat_s = jnp.transpose(qlat.reshape(H, NB, T, lkv), (1, 2, 0, 3)).reshape(NB, T * H, lkv)
    qrope_s = qr_all[RP:].reshape(NB, T, H, LANES)[..., :rope].reshape(NB, T * H, rope)
    tp = 8
    cnew = jnp.pad(c_all[RP:].reshape(NB, T, lkv), ((0, 0), (0, tp - T), (0, 0)))
    krnew = jnp.pad(kr_all[RP:].reshape(NB, T, rope), ((0, 0), (0, tp - T), (0, 0)))

    npp = PAGES_PER_STEP
    nstep = n_pages // npp
    nrow = T * H

    def page_spec(width, kk):
        return pl.BlockSpec((None, None, page, width),
                            lambda s, j, pt: (0, pt[s, j * npp + kk], 0, 0))

    seq_spec = lambda r, w: pl.BlockSpec((None, r, w), lambda s, j, pt: (s, 0, 0))
    olat = pl.pallas_call(
        functools.partial(_mla_sample_attn_kernel, npp=npp, t_new=T, heads=H, scale=mla_scale),
        grid_spec=pltpu.PrefetchScalarGridSpec(
            num_scalar_prefetch=1, grid=(NB, nstep),
            in_specs=[seq_spec(nrow, lkv), seq_spec(nrow, rope), seq_spec(tp, lkv), seq_spec(tp, rope)]
                     + [page_spec(lkv, kk) for kk in range(npp)] + [page_spec(rope, kk) for kk in range(npp)],
            out_specs=seq_spec(nrow, lkv),
            scratch_shapes=[pltpu.VMEM((nrow, 1), F32), pltpu.VMEM((nrow, 1), F32), pltpu.VMEM((nrow, lkv), F32)]),
        out_shape=jax.ShapeDtypeStruct((NB, nrow, lkv), BF16),
        compiler_params=_cparams("parallel", "arbitrary"), name="mla_sample_attn",
    )(page_table, qlat_s, qrope_s, cnew, krnew, *([cache_mla_latent] * npp), *([cache_mla_krope] * npp))
    olat_h = jnp.transpose(olat.reshape(NB, T, H, lkv), (2, 0, 1, 3)).reshape(H, RS, lkv)
    ogs0 = pl.pallas_call(
        _mla_sample_out_kernel,
        grid=(H,),
        in_specs=[pl.BlockSpec((None, RS, lkv), lambda h: (h, 0, 0)),
                  pl.BlockSpec((lkv, vd), lambda h: (0, h)),
                  pl.BlockSpec((RS, vd), lambda h: (sblk, h))],
        out_specs=pl.BlockSpec((RS, vd), lambda h: (0, h)),
        out_shape=jax.ShapeDtypeStruct((RS, nbr), BF16),
        compiler_params=_cparams("parallel"), name="mla_sample_out",
    )(olat_h, w_uv, gate0)

    npb = RP // tm
    nsb = RS // tm
    h1 = pl.pallas_call(
        functools.partial(_outproj_kernel, n_prompt_blocks=npb),
        grid=(nblk,),
        in_specs=[row_spec(D),
                  pl.BlockSpec((tm, nbr), lambda i: (jnp.minimum(i, npb - 1), 0)),
                  pl.BlockSpec((tm, nbr), lambda i: (jnp.maximum(i - npb, 0), 0)),
                  _const_spec((nbr, D))],
        out_specs=row_spec(D),
        out_shape=jax.ShapeDtypeStruct((R, D), F32),
        compiler_params=_cparams("parallel"), name="outproj0",
    )(h0, ogp0, ogs0, w_out0)

    sbq, sbk, sbv, gate1 = pl.pallas_call(
        functools.partial(_sb_in_kernel, nbr=sb_nbr, kvw=kvw),
        grid=(nblk,),
        in_specs=[row_spec(D), _const_spec((1, D)), _const_spec((D, 2 * sb_nbr + 2 * kvw))],
        out_specs=[row_spec(sb_nbr), row_spec(kvw), row_spec(kvw), row_spec(sb_nbr)],
        out_shape=[jax.ShapeDtypeStruct((R, sb_nbr), BF16), jax.ShapeDtypeStruct((R, kvw), F32),
                   jax.ShapeDtypeStruct((R, kvw), F32), jax.ShapeDtypeStruct((R, sb_nbr), F32)],
        compiler_params=_cparams("parallel"), name="sb_in",
    )(h1, g1, w_sb_in)

    kb = SB_KEY_BLOCK
    tri = (lax.broadcasted_iota(jnp.int32, (kb, kb), 0) >= lax.broadcasted_iota(jnp.int32, (kb, kb), 1)).astype(BF16)
    sb_scale = float(hd) ** -0.5
    kv_spec = pl.BlockSpec((LP, hd), lambda b, h: (b, h // HG))
    ogp1 = pl.pallas_call(
        functools.partial(_sb_prompt_attn_kernel, chunks=chunks, pad=pad, scale=sb_scale, kb=kb),
        grid=(B, SH),
        in_specs=[head_spec, kv_spec, kv_spec, head_spec, pl.BlockSpec((kb, kb), lambda b, h: (0, 0))],
        out_specs=head_spec,
        out_shape=jax.ShapeDtypeStruct((RP, sb_nbr), BF16),
        compiler_params=_cparams("parallel", "parallel"), name="sb_prompt_attn",
    )(sbq, sbk, sbv, gate1, tri)

    rpg = T * HG
    q_s = jnp.transpose(sbq[RP:].reshape(NB, T, G, HG, hd), (0, 2, 1, 3, 4)).reshape(NB, G, rpg, hd)
    knew = jnp.pad(sbk[RP:].reshape(NB, T, kvw), ((0, 0), (0, tp - T), (0, 0)))
    vnew = jnp.pad(sbv[RP:].reshape(NB, T, kvw), ((0, 0), (0, tp - T), (0, 0)))
    ck = cache_sb_k.reshape(cache_sb_k.shape[:3] + (kvw,))
    cv = cache_sb_v.reshape(cache_sb_v.shape[:3] + (kvw,))

    def rpage_spec(kk):
        return pl.BlockSpec((None, None, page, kvw),
                            lambda s, j, pt: (0, pt[s, n_pages - 1 - (j * npp + kk)], 0, 0))

    o_s = pl.pallas_call(
        functools.partial(_sb_sample_attn_kernel, npp=npp, groups=G, hd=hd, scale=sb_scale, t_new=T),
        grid_spec=pltpu.PrefetchScalarGridSpec(
            num_scalar_prefetch=1, grid=(NB, nstep),
            in_specs=[pl.BlockSpec((None, G, rpg, hd), lambda s, j, pt: (s, 0, 0, 0)),
                      seq_spec(tp, kvw), seq_spec(tp, kvw),
                      pl.BlockSpec((page, page), lambda s, j, pt: (0, 0))]
                     + [rpage_spec(kk) for kk in range(npp)] + [rpage_spec(kk) for kk in range(npp)],
            out_specs=seq_spec(G * rpg, hd),
            scratch_shapes=[pltpu.VMEM((G * rpg, LANES), F32), pltpu.VMEM((G * rpg, hd), F32)]),
        out_shape=jax.ShapeDtypeStruct((NB, G * rpg, hd), F32),
        compiler_params=_cparams("parallel", "arbitrary"), name="sb_sample_attn",
    )(page_table, q_s, knew, vnew, tri[:page, :page], *([ck] * npp), *([cv] * npp))
    o_tok = jnp.transpose(o_s.reshape(NB, G, T, HG, hd), (0, 2, 1, 3, 4)).reshape(RS, sb_nbr)
    ogs1 = pl.pallas_call(
        _gate_kernel,
        grid=(1,),
        in_specs=[pl.BlockSpec((RS, sb_nbr), lambda i: (0, 0)), pl.BlockSpec((RS, sb_nbr), lambda i: (sblk, 0))],
        out_specs=pl.BlockSpec((RS, sb_nbr), lambda i: (0, 0)),
        out_shape=jax.ShapeDtypeStruct((RS, sb_nbr), BF16),
        compiler_params=_cparams("arbitrary"), name="sb_sample_gate",
    )(o_tok, gate1)

    bpl = LP // ROW_ALIGN
    spb = S // ROW_ALIGN
    tf = ROW_ALIGN
    y_prompt = pl.pallas_call(
        _outproj_final_kernel,
        grid=(B, spb),
        in_specs=[pl.BlockSpec((tf, D), lambda b, i: (b * bpl + (bpl - spb) + i, 0)),
                  pl.BlockSpec((tf, sb_nbr), lambda b, i: (b * bpl + (bpl - spb) + i, 0)),
                  _const_spec((sb_nbr, D)), _const_spec((1, D))],
        out_specs=pl.BlockSpec((None, tf, D), lambda b, i: (b, i, 0)),
        out_shape=jax.ShapeDtypeStruct((B, S, D), F32),
        compiler_params=_cparams("parallel", "parallel"), name="outproj1_prompt",
    )(h1, ogp1, w_out1, gf)
    y_sample = pl.pallas_call(
        _outproj_final_kernel,
        grid=(RS // tf,),
        in_specs=[pl.BlockSpec((tf, D), lambda i: (RP // tf + i, 0)),
                  pl.BlockSpec((tf, sb_nbr), lambda i: (i, 0)),
                  _const_spec((sb_nbr, D)), _const_spec((1, D))],
        out_specs=pl.BlockSpec((tf, D), lambda i: (i, 0)),
        out_shape=jax.ShapeDtypeStruct((RS, D), F32),
        compiler_params=_cparams("parallel"), name="outproj1_sample",
    )(h1, ogs1, w_out1, gf).reshape(NB, T, D)

    def prompt_rows(a):
        return a[:RP].reshape((B, LP) + a.shape[1:])[:, pad:][None]

    def sample_rows(a):
        return a[RP:].reshape((NB, T) + a.shape[1:])[None]

    sbk4 = sbk.reshape(R, G, hd)
    sbv4 = sbv.reshape(R, G, hd)
    return (y_prompt, y_sample,
            prompt_rows(c_all), prompt_rows(kr_all), prompt_rows(sbk4), prompt_rows(sbv4),
            sample_rows(c_all), sample_rows(kr_all), sample_rows(sbk4), sample_rows(sbv4))
```

```python
import functools

import jax
import jax.numpy as jnp
from jax import lax
from jax.experimental import pallas as pl
from jax.experimental.pallas import tpu as pltpu

EPS = 1e-6
NEG_INF = -1e30
ROPE_THETA = 10000.0
LANES = 128
BF16_ROWS = 16
ROW_ALIGN = 128
VMEM_LIMIT = 56 * 1024 * 1024
ROW_TILE = 512
COL_TILE = 1024
QUERY_CHUNK = 512
PAGES_PER_CHUNK = 16
PAGES_PER_UPDATE = 16
SB_KEY_BLOCK = 256

F32 = jnp.float32
BF16 = jnp.bfloat16


def _cparams(*sem):
    return pltpu.CompilerParams(dimension_semantics=sem, vmem_limit_bytes=VMEM_LIMIT)


def _rms(x, g):
    return x * lax.rsqrt(jnp.mean(x * x, axis=-1, keepdims=True) + EPS) * g


def _dot(a, b):
    return jnp.dot(a, b, preferred_element_type=F32)


def _dot_nt(a, b):
    return lax.dot_general(a, b, (((1,), (1,)), ((), ())), preferred_element_type=F32)


def _silu(x):
    return x * (1.0 / (1.0 + jnp.exp(-x)))


def _softplus(z):
    return jnp.maximum(z, 0.0) + jnp.log(1.0 + jnp.exp(-jnp.abs(z)))


def _const_spec(shape):
    nd = len(shape)
    return pl.BlockSpec(shape, lambda *_: (0,) * nd, pipeline_mode=pl.Buffered(1))


def _col_tiles(n):
    return [(a, min(a + COL_TILE, n)) for a in range(0, n, COL_TILE)]


def _mla_in_kernel(h_ref, g_ref, w_ref, gq_ref, gkv_ref, cs_ref, sn_ref,
                   cqn_ref, c_ref, kr_ref, krp_ref, gate_ref, *, lq, lkv, nbr, rope):
    xn = _rms(h_ref[...], g_ref[...]).astype(BF16)
    cqn_ref[...] = _rms(_dot(xn, w_ref[:, :lq]), gq_ref[...]).astype(BF16)
    c_ref[...] = _rms(_dot(xn, w_ref[:, lq:lq + lkv]), gkv_ref[...])
    o = lq + lkv
    for a, b in _col_tiles(nbr):
        gate_ref[:, a:b] = _dot(xn, w_ref[:, o + a:o + b])
    o += nbr
    u = _dot(xn, w_ref[:, o:o + 2 * LANES])
    kr = u[:, :LANES] * cs_ref[...] + u[:, LANES:] * sn_ref[...]
    kr_ref[...] = kr[:, :rope]
    krp_ref[...] = kr.astype(BF16)


def _mla_up_kernel(cqn_ref, c_ref, cs_ref, sn_ref, wqn_ref, wqa_ref, wqb_ref, wuk_ref, wuv_ref,
                   qn_ref, qr_ref, kn_ref, v_ref, *, heads):
    cqn = cqn_ref[...]
    c16 = c_ref[...].astype(BF16)
    cs = jnp.concatenate([cs_ref[...]] * (COL_TILE // LANES), axis=1)
    sn = jnp.concatenate([sn_ref[...]] * (COL_TILE // LANES), axis=1)
    for a, b in _col_tiles(heads * LANES):
        qn_ref[:, a:b] = _dot(cqn, wqn_ref[:, a:b]).astype(BF16)
        qr_ref[:, a:b] = (_dot(cqn, wqa_ref[:, a:b]) * cs[:, :b - a]
                          + _dot(cqn, wqb_ref[:, a:b]) * sn[:, :b - a]).astype(BF16)
        kn_ref[:, a:b] = _dot(c16, wuk_ref[:, a:b]).astype(BF16)
        v_ref[:, a:b] = _dot(c16, wuv_ref[:, a:b]).astype(BF16)


def _mla_prompt_attn_kernel(qn_ref, qr_ref, kn_ref, kr_ref, v_ref, gate_ref, o_ref, *, chunks, pad, scale):
    def keys(a, b):
        return jnp.concatenate([kn_ref[a:b, :], kr_ref[a:b, :]], axis=-1)

    for r0, r1 in chunks:
        nq = r1 - r0
        q = jnp.concatenate([qn_ref[r0:r1, :], qr_ref[r0:r1, :]], axis=-1)
        d0 = max(r0, pad)
        s_d = _dot_nt(q, keys(d0, r1)) * scale
        row = r0 + lax.broadcasted_iota(jnp.int32, (nq, r1 - d0), 0)
        col = d0 + lax.broadcasted_iota(jnp.int32, (nq, r1 - d0), 1)
        s_d = jnp.where(col <= row, s_d, NEG_INF)
        m = jnp.max(s_d, axis=-1, keepdims=True)
        if d0 > pad:
            s_f = _dot_nt(q, keys(pad, d0)) * scale
            m = jnp.maximum(m, jnp.max(s_f, axis=-1, keepdims=True))
        p_d = jnp.exp(s_d - m)
        l = jnp.sum(p_d, axis=-1, keepdims=True)
        o = _dot(p_d.astype(BF16), v_ref[d0:r1, :])
        if d0 > pad:
            p_f = jnp.exp(s_f - m)
            l = l + jnp.sum(p_f, axis=-1, keepdims=True)
            o = o + _dot(p_f.astype(BF16), v_ref[pad:d0, :])
        o_ref[r0:r1, :] = (o * (1.0 / l) * _silu(gate_ref[r0:r1, :])).astype(BF16)


def _page_copies(pt_ref, hbm_refs, bufs, sems, seq, first_page, slot, npp, lookup):
    out = []
    for kk in range(npp):
        pid = pt_ref[seq, first_page + kk] if lookup else 0
        for a, (hbm, buf) in enumerate(zip(hbm_refs, bufs)):
            out.append(pltpu.make_async_copy(hbm.at[0, pid], buf.at[slot, kk], sems.at[a, slot]))
    return out


def _ring_loop(pt_ref, hbm_refs, bufs, sems, nchunk, npp, first_page_of, compute):
    seq = pl.program_id(0)
    nseq = pl.num_programs(0)

    def start(sq, chunk, slot):
        for cp in _page_copies(pt_ref, hbm_refs, bufs, sems, sq, first_page_of(chunk), slot, npp, True):
            cp.start()

    @pl.when(seq == 0)
    def _():
        start(seq, 0, 0)

    def body(c, carry):
        slot = lax.rem(seq * nchunk + c, 2)

        @pl.when(c + 1 < nchunk)
        def _():
            start(seq, c + 1, 1 - slot)

        @pl.when((c + 1 == nchunk) & (seq + 1 < nseq))
        def _():
            start(seq + 1, 0, 1 - slot)

        for cp in _page_copies(pt_ref, hbm_refs, bufs, sems, seq, 0, slot, npp, False):
            cp.wait()
        compute(slot)
        return carry

    lax.fori_loop(0, nchunk, body, 0)


def _mla_qlat_kernel(qn_ref, wukt_ref, o_ref):
    o_ref[...] = _dot(qn_ref[...], wukt_ref[...]).astype(BF16)


def _token_of_row(row, rows_per_token, n_tokens):
    t = jnp.zeros(row.shape, jnp.int32)
    for i in range(1, n_tokens):
        t = t + (row >= i * rows_per_token).astype(jnp.int32)
    return t


def _mla_sample_attn_kernel(pt_ref, ql_ref, qr_ref, cnew_ref, krnewt_ref, lat_hbm, krt_hbm, o_ref,
                            lat_buf, krt_buf, sems, m_ref, l_ref, acc_ref,
                            *, npp, nchunk, sub, t_new, heads, scale):
    ql = ql_ref[...]
    qr = qr_ref[...]
    nrow = ql.shape[0]
    page = lat_buf.shape[2]
    lkv = lat_buf.shape[3]

    def update(c16, krt16, mask):
        s = (_dot_nt(ql, c16) + _dot(qr, krt16)) * scale
        if mask is not None:
            s = jnp.where(mask, s, NEG_INF)
        m_old = m_ref[...]
        m_new = jnp.maximum(m_old, jnp.max(s, axis=-1, keepdims=True))
        alpha = jnp.exp(m_old - m_new)
        p = jnp.exp(s - m_new)
        l_ref[...] = l_ref[...] * alpha + jnp.sum(p, axis=-1, keepdims=True)
        acc_ref[...] = acc_ref[...] * alpha + _dot(p.astype(BF16), c16)
        m_ref[...] = m_new

    m_ref[...] = jnp.full(m_ref.shape, NEG_INF, F32)
    l_ref[...] = jnp.zeros(l_ref.shape, F32)
    acc_ref[...] = jnp.zeros(acc_ref.shape, F32)
    nn = cnew_ref.shape[0]
    cnew16 = jnp.concatenate([cnew_ref[...], jnp.zeros((page - nn, lkv), F32)], axis=0).astype(BF16)
    tq = _token_of_row(lax.broadcasted_iota(jnp.int32, (nrow, page), 0), heads, t_new)
    col = lax.broadcasted_iota(jnp.int32, (nrow, page), 1)
    update(cnew16, krnewt_ref[...].astype(BF16), col <= tq)

    def compute(slot):
        for i in range(npp // sub):
            c16 = lat_buf[slot, i * sub:(i + 1) * sub].reshape(sub * page, lkv).astype(BF16)
            krt16 = jnp.concatenate([krt_buf[slot, i * sub + k].astype(BF16) for k in range(sub)], axis=1)
            update(c16, krt16, None)

    _ring_loop(pt_ref, (lat_hbm, krt_hbm), (lat_buf, krt_buf), sems, nchunk, npp,
               lambda c: c * npp, compute)
    o_ref[...] = (acc_ref[...] * (1.0 / l_ref[...])).astype(BF16)


def _mla_sample_out_kernel(ol_ref, wuv_ref, gate_ref, o_ref):
    o = _dot(ol_ref[...], wuv_ref[...])
    o_ref[...] = (o * _silu(gate_ref[...])).astype(BF16)


def _outproj_kernel(h_ref, ogp_ref, ogs_ref, w_ref, o_ref, *, n_prompt_blocks):
    i = pl.program_id(0)

    @pl.when(i < n_prompt_blocks)
    def _():
        o_ref[...] = h_ref[...] + _dot(ogp_ref[...], w_ref[...])

    @pl.when(i >= n_prompt_blocks)
    def _():
        o_ref[...] = h_ref[...] + _dot(ogs_ref[...], w_ref[...])


def _sb_in_kernel(h_ref, g_ref, w_ref, q_ref, k_ref, v_ref, gate_ref, *, nbr, kvw):
    xn = _rms(h_ref[...], g_ref[...]).astype(BF16)
    for a, b in _col_tiles(nbr):
        q_ref[:, a:b] = _dot(xn, w_ref[:, a:b]).astype(BF16)
    k_ref[...] = _dot(xn, w_ref[:, nbr:nbr + kvw])
    v_ref[...] = _dot(xn, w_ref[:, nbr + kvw:nbr + 2 * kvw])
    o = nbr + 2 * kvw
    for a, b in _col_tiles(nbr):
        gate_ref[:, a:b] = _dot(xn, w_ref[:, o + a:o + b])


def _outproj_final_kernel(h_ref, og_ref, w_ref, g_ref, y_ref):
    h = h_ref[...] + _dot(og_ref[...], w_ref[...])
    y_ref[...] = _rms(h, g_ref[...])


def _suffix_sums(lk, tri):
    hi = lk.astype(BF16)
    lo = (lk - hi.astype(F32)).astype(BF16)
    return _dot(hi, tri) + _dot(lo, tri)


def _sb_prompt_attn_kernel(q_ref, k_ref, v_ref, gate_ref, tri_ref, o_ref, *, chunks, pad, scale, kb):
    hd = q_ref.shape[1]
    tri = tri_ref[...]
    for r0, r1 in chunks:
        nq = r1 - r0
        ext = (-r1) % kb
        nblk = (r1 + ext) // kb
        k16 = k_ref[0:r1, :].astype(BF16)
        v16 = v_ref[0:r1, :].astype(BF16)
        if ext:
            k16 = jnp.concatenate([jnp.zeros((ext, hd), BF16), k16], axis=0)
            v16 = jnp.concatenate([jnp.zeros((ext, hd), BF16), v16], axis=0)
        z = _dot_nt(q_ref[r0:r1, :], k16) * scale
        lk = -_softplus(z)
        masks = [None] * nblk
        lk_blocks = []
        for b in range(nblk):
            c0 = b * kb - ext
            blk = lk[:, b * kb:(b + 1) * kb]
            if not (c0 + kb <= r0 and c0 >= pad):
                row = r0 + lax.broadcasted_iota(jnp.int32, (nq, kb), 0)
                col = c0 + lax.broadcasted_iota(jnp.int32, (nq, kb), 1)
                masks[b] = (col < row) & (col >= pad)
                blk = jnp.where(masks[b], blk, 0.0)
            lk_blocks.append(blk)
        ssum = _suffix_sums(jnp.concatenate(lk_blocks, axis=0), tri)
        carry = jnp.zeros((nq, 1), F32)
        a_blocks = [None] * nblk
        for b in reversed(range(nblk)):
            sb = ssum[b * nq:(b + 1) * nq, :]
            a = jnp.exp(z[:, b * kb:(b + 1) * kb] + sb + carry)
            if masks[b] is not None:
                a = jnp.where(masks[b], a, 0.0)
            a_blocks[b] = a.astype(BF16)
            carry = carry + sb[:, 0:1]
        o = _dot(jnp.concatenate(a_blocks, axis=1), v16)
        o_ref[r0:r1, :] = (o * _silu(gate_ref[r0:r1, :])).astype(BF16)


def _sb_sample_attn_kernel(pt_ref, q_ref, knew_ref, vnew_ref, tri_ref, k_hbm, v_hbm, o_ref,
                           k_buf, v_buf, sems, carry_ref, acc_ref,
                           *, npp, nchunk, n_pages, groups, scale, t_new, kb):
    rpg = q_ref.shape[1]
    hd = q_ref.shape[2]
    page = k_buf.shape[2] // groups
    nrow = groups * rpg
    qs = [q_ref[g] for g in range(groups)]
    tri = tri_ref[...]

    def scores(kg):
        return jnp.concatenate([_dot_nt(qs[g], kg[g]) for g in range(groups)], axis=0) * scale

    def weighted(a, vg):
        a16 = a.astype(BF16)
        return jnp.concatenate([_dot(a16[g * rpg:(g + 1) * rpg, :], vg[g]) for g in range(groups)], axis=0)

    nn = knew_ref.shape[0]
    zpad = jnp.zeros((page - nn, hd), F32)
    kg = [jnp.concatenate([knew_ref[:, g * hd:(g + 1) * hd], zpad], axis=0).astype(BF16) for g in range(groups)]
    vg = [jnp.concatenate([vnew_ref[:, g * hd:(g + 1) * hd], zpad], axis=0).astype(BF16) for g in range(groups)]
    z = scores(kg)
    tq1 = _token_of_row(lax.broadcasted_iota(jnp.int32, (rpg, page), 0), rpg // t_new, t_new)
    tq = jnp.concatenate([tq1] * groups, axis=0)
    mask = lax.broadcasted_iota(jnp.int32, (nrow, page), 1) < tq
    lk = jnp.where(mask, -_softplus(z), 0.0)
    ssum = _suffix_sums(lk, tri_ref[0:page, 0:page])
    acc_ref[...] = weighted(jnp.where(mask, jnp.exp(z + ssum), 0.0), vg)
    carry_ref[...] = jnp.broadcast_to(ssum[:, 0:1], carry_ref.shape)

    nkb = npp * page // kb

    def compute(slot):
        kg = [jnp.concatenate([k_buf[slot, kk, pl.ds(g, page, stride=groups), :] for kk in range(npp)],
                              axis=0).astype(BF16) for g in range(groups)]
        z = scores(kg)
        lk = -_softplus(z)
        lk_blocks = jnp.concatenate([lk[:, b * kb:(b + 1) * kb] for b in range(nkb)], axis=0)
        ssum = _suffix_sums(lk_blocks, tri)
        carry = carry_ref[:, 0:1]
        a_blocks = [None] * nkb
        for b in reversed(range(nkb)):
            sb = ssum[b * nrow:(b + 1) * nrow, :]
            a_blocks[b] = jnp.exp(z[:, b * kb:(b + 1) * kb] + sb + carry)
            carry = carry + sb[:, 0:1]
        vg = [jnp.concatenate([v_buf[slot, kk, pl.ds(g, page, stride=groups), :] for kk in range(npp)],
                              axis=0).astype(BF16) for g in range(groups)]
        acc_ref[...] += weighted(jnp.concatenate(a_blocks, axis=1), vg)
        carry_ref[...] = jnp.broadcast_to(carry, carry_ref.shape)

    _ring_loop(pt_ref, (k_hbm, v_hbm), (k_buf, v_buf), sems, nchunk, npp,
               lambda c: n_pages - (c + 1) * npp, compute)
    o_ref[...] = acc_ref[...]


def _gate_kernel(o_ref, gate_ref, og_ref):
    og_ref[...] = (o_ref[...] * _silu(gate_ref[...])).astype(BF16)


def _query_chunks(lp):
    n = max(lp // QUERY_CHUNK, 1)
    bounds = [i * QUERY_CHUNK for i in range(n)] + [lp]
    return tuple((bounds[i], bounds[i + 1]) for i in range(n))


def kernel(x_prompt, x_sample, cache_mla_latent, cache_mla_krope, cache_sb_k, cache_sb_v, page_table,
           meta_tokens, mla_norm, mla_w_in, mla_q_norm, mla_kv_norm, mla_w_uq, mla_w_uk, mla_w_uv, mla_w_out,
           sb_norm, sb_w_in, sb_w_out, final_norm):
    B, S, D = x_prompt.shape
    NB, T = x_sample.shape[:2]
    n_meta = meta_tokens.shape[0]
    n_pages = page_table.shape[1]
    page = cache_mla_latent.shape[2]
    lq = mla_q_norm.shape[-1]
    lkv = mla_kv_norm.shape[-1]
    H = mla_w_uq.shape[2]
    nope = mla_w_uk.shape[-1]
    rope = mla_w_uq.shape[-1] - nope
    vd = mla_w_uv.shape[-1]
    nbr = H * vd
    half = rope // 2
    G = cache_sb_k.shape[3]
    hd = cache_sb_k.shape[4]
    kvw = G * hd
    sb_nbr = sb_w_out.shape[1]
    SH = sb_nbr // hd
    HG = SH // G
    npp = PAGES_PER_CHUNK
    assert nope == LANES and vd == LANES and hd == LANES and page == LANES and 2 * rope == LANES
    assert mla_norm.shape[0] == 1 and sb_norm.shape[0] == 1, "one layer of each mixer"
    assert n_pages % npp == 0 and npp % PAGES_PER_UPDATE == 0 and (npp * page) % SB_KEY_BLOCK == 0 and T <= 8

    L = n_meta + S
    pad = (-L) % ROW_ALIGN
    LP = L + pad
    RP = B * LP
    RS = NB * T
    R = RP + RS
    tm = ROW_TILE
    assert RP % tm == 0 and RS % tm == 0 and RP % RS == 0 and S % ROW_ALIGN == 0 and pad % BF16_ROWS == 0
    nblk = R // tm
    past = n_pages * page

    hp = jnp.concatenate([jnp.zeros((B, pad, D), F32),
                          jnp.broadcast_to(meta_tokens.astype(F32)[None], (B, n_meta, D)), x_prompt], axis=1)
    h0 = jnp.concatenate([hp.reshape(RP, D), x_sample.reshape(RS, D)], axis=0)

    pos = jnp.concatenate([jnp.tile(jnp.arange(LP, dtype=jnp.int32) - pad, B),
                           jnp.tile(past + jnp.arange(T, dtype=jnp.int32), NB)])
    inv = ROPE_THETA ** (-jnp.arange(half, dtype=F32) / half)
    ang = pos.astype(F32)[:, None] * inv[None, :]
    zpadl = jnp.zeros((R, LANES - rope), F32)
    cs = jnp.concatenate([jnp.cos(ang), jnp.cos(ang), zpadl], axis=1)
    sn = jnp.concatenate([jnp.sin(ang), jnp.sin(ang), zpadl], axis=1)

    def rot_cols(w):
        return jnp.concatenate([-w[..., half:], w[..., :half]], axis=-1)

    def lane_pad(w):
        return jnp.concatenate([w, jnp.zeros(w.shape[:-1] + (LANES - w.shape[-1],), w.dtype)], axis=-1)

    w_in = mla_w_in[0]
    o_kr = lq + lkv
    w_kr = w_in[:, o_kr:o_kr + rope]
    w_in_ext = jnp.concatenate([w_in[:, :o_kr], w_in[:, o_kr + rope:], lane_pad(w_kr), lane_pad(rot_cols(w_kr))],
                               axis=1).astype(BF16)
    n_in = w_in_ext.shape[1]
    w_uq = mla_w_uq[0]
    w_qn = w_uq[:, :, :nope].reshape(lq, H * nope).astype(BF16)
    w_qa = lane_pad(w_uq[:, :, nope:]).reshape(lq, H * LANES).astype(BF16)
    w_qb = lane_pad(rot_cols(w_uq[:, :, nope:])).reshape(lq, H * LANES).astype(BF16)
    w_uk = mla_w_uk[0].reshape(lkv, H * nope).astype(BF16)
    w_uv = mla_w_uv[0].reshape(lkv, H * vd).astype(BF16)
    w_ukt = jnp.transpose(mla_w_uk[0], (1, 2, 0)).astype(BF16)
    w_out0 = mla_w_out[0].astype(BF16)
    w_sb_in = sb_w_in[0].astype(BF16)
    w_out1 = sb_w_out[0].astype(BF16)
    g0 = mla_norm[0].reshape(1, D)
    gq = mla_q_norm[0].reshape(1, lq)
    gkv = mla_kv_norm[0].reshape(1, lkv)
    g1 = sb_norm[0].reshape(1, D)
    gf = final_norm.reshape(1, D)

    row_spec = lambda w: pl.BlockSpec((tm, w), lambda i: (i, 0))

    cqn, c_all, kr_all, krp_all, gate0 = pl.pallas_call(
        functools.partial(_mla_in_kernel, lq=lq, lkv=lkv, nbr=nbr, rope=rope),
        grid=(nblk,),
        in_specs=[row_spec(D), _const_spec((1, D)), _const_spec((D, n_in)), _const_spec((1, lq)),
                  _const_spec((1, lkv)), row_spec(LANES), row_spec(LANES)],
        out_specs=[row_spec(lq), row_spec(lkv), row_spec(rope), row_spec(LANES), row_spec(nbr)],
        out_shape=[jax.ShapeDtypeStruct((R, lq), BF16), jax.ShapeDtypeStruct((R, lkv), F32),
                   jax.ShapeDtypeStruct((R, rope), F32), jax.ShapeDtypeStruct((R, LANES), BF16),
                   jax.ShapeDtypeStruct((R, nbr), F32)],
        compiler_params=_cparams("arbitrary"), name="mla_in",
    )(h0, g0, w_in_ext, gq, gkv, cs, sn)

    qn_all, qr_all, kn_all, v_all = pl.pallas_call(
        functools.partial(_mla_up_kernel, heads=H),
        grid=(nblk,),
        in_specs=[row_spec(lq), row_spec(lkv), row_spec(LANES), row_spec(LANES),
                  _const_spec((lq, H * nope)), _const_spec((lq, H * LANES)), _const_spec((lq, H * LANES)),
                  _const_spec((lkv, H * nope)), _const_spec((lkv, H * vd))],
        out_specs=[row_spec(H * nope), row_spec(H * LANES), row_spec(H * nope), row_spec(H * vd)],
        out_shape=[jax.ShapeDtypeStruct((R, H * nope), BF16), jax.ShapeDtypeStruct((R, H * LANES), BF16),
                   jax.ShapeDtypeStruct((R, H * nope), BF16), jax.ShapeDtypeStruct((R, H * vd), BF16)],
        compiler_params=_cparams("arbitrary"), name="mla_up",
    )(cqn, c_all, cs, sn, w_qn, w_qa, w_qb, w_uk, w_uv)

    chunks = _query_chunks(LP)
    mla_scale = float(nope + rope) ** -0.5
    head_spec = pl.BlockSpec((LP, LANES), lambda b, h: (b, h))
    ogp0 = pl.pallas_call(
        functools.partial(_mla_prompt_attn_kernel, chunks=chunks, pad=pad, scale=mla_scale),
        grid=(B, H),
        in_specs=[head_spec, head_spec, head_spec, pl.BlockSpec((LP, LANES), lambda b, h: (b, 0)),
                  head_spec, head_spec],
        out_specs=head_spec,
        out_shape=jax.ShapeDtypeStruct((RP, nbr), BF16),
        compiler_params=_cparams("arbitrary", "arbitrary"), name="mla_prompt_attn",
    )(qn_all, qr_all, kn_all, krp_all, v_all, gate0)

    sblk = RP // RS
    qlat = pl.pallas_call(
        _mla_qlat_kernel,
        grid=(H,),
        in_specs=[pl.BlockSpec((RS, nope), lambda h: (sblk, h)),
                  pl.BlockSpec((None, nope, lkv), lambda h: (h, 0, 0))],
        out_specs=pl.BlockSpec((None, RS, lkv), lambda h: (h, 0, 0)),
        out_shape=jax.ShapeDtypeStruct((H, RS, lkv), BF16),
        compiler_params=_cparams("arbitrary"), name="mla_qlat",
    )(qn_all, w_ukt)
    qlat_s = jnp.transpose(qlat.reshape(H, NB, T, lkv), (1, 2, 0, 3)).reshape(NB, T * H, lkv)
    qrope_s = qr_all[RP:].reshape(NB, T, H, LANES)[..., :rope].reshape(NB, T * H, rope)
    tp = 8
    cnew = jnp.pad(c_all[RP:].reshape(NB, T, lkv), ((0, 0), (0, tp - T), (0, 0)))
    krnew_t = jnp.pad(jnp.swapaxes(kr_all[RP:].reshape(NB, T, rope), 1, 2), ((0, 0), (0, 0), (0, page - T)))
    krope_t = jnp.swapaxes(cache_mla_krope, 2, 3)

    nchunk = n_pages // npp
    nrow = T * H
    seq_spec = lambda r, w: pl.BlockSpec((None, r, w), lambda s, pt: (s, 0, 0))
    any_spec = pl.BlockSpec(memory_space=pl.ANY)
    olat = pl.pallas_call(
        functools.partial(_mla_sample_attn_kernel, npp=npp, nchunk=nchunk, sub=PAGES_PER_UPDATE,
                          t_new=T, heads=H, scale=mla_scale),
        grid_spec=pltpu.PrefetchScalarGridSpec(
            num_scalar_prefetch=1, grid=(NB,),
            in_specs=[seq_spec(nrow, lkv), seq_spec(nrow, rope), seq_spec(tp, lkv), seq_spec(rope, page),
                      any_spec, any_spec],
            out_specs=seq_spec(nrow, lkv),
            scratch_shapes=[pltpu.VMEM((2, npp, page, lkv), F32), pltpu.VMEM((2, npp, rope, page), F32),
                            pltpu.SemaphoreType.DMA((2, 2)),
                            pltpu.VMEM((nrow, 1), F32), pltpu.VMEM((nrow, 1), F32), pltpu.VMEM((nrow, lkv), F32)]),
        out_shape=jax.ShapeDtypeStruct((NB, nrow, lkv), BF16),
        compiler_params=_cparams("arbitrary"), name="mla_sample_attn",
    )(page_table, qlat_s, qrope_s, cnew, krnew_t, cache_mla_latent, krope_t)
    olat_h = jnp.transpose(olat.reshape(NB, T, H, lkv), (2, 0, 1, 3)).reshape(H, RS, lkv)
    ogs0 = pl.pallas_call(
        _mla_sample_out_kernel,
        grid=(H,),
        in_specs=[pl.BlockSpec((None, RS, lkv), lambda h: (h, 0, 0)),
                  pl.BlockSpec((lkv, vd), lambda h: (0, h)),
                  pl.BlockSpec((RS, vd), lambda h: (sblk, h))],
        out_specs=pl.BlockSpec((RS, vd), lambda h: (0, h)),
        out_shape=jax.ShapeDtypeStruct((RS, nbr), BF16),
        compiler_params=_cparams("arbitrary"), name="mla_sample_out",
    )(olat_h, w_uv, gate0)

    npb = RP // tm
    h1 = pl.pallas_call(
        functools.partial(_outproj_kernel, n_prompt_blocks=npb),
        grid=(nblk,),
        in_specs=[row_spec(D),
                  pl.BlockSpec((tm, nbr), lambda i: (jnp.minimum(i, npb - 1), 0)),
                  pl.BlockSpec((tm, nbr), lambda i: (jnp.maximum(i - npb, 0), 0)),
                  _const_spec((nbr, D))],
        out_specs=row_spec(D),
        out_shape=jax.ShapeDtypeStruct((R, D), F32),
        compiler_params=_cparams("arbitrary"), name="outproj0",
    )(h0, ogp0, ogs0, w_out0)

    sbq, sbk, sbv, gate1 = pl.pallas_call(
        functools.partial(_sb_in_kernel, nbr=sb_nbr, kvw=kvw),
        grid=(nblk,),
        in_specs=[row_spec(D), _const_spec((1, D)), _const_spec((D, 2 * sb_nbr + 2 * kvw))],
        out_specs=[row_spec(sb_nbr), row_spec(kvw), row_spec(kvw), row_spec(sb_nbr)],
        out_shape=[jax.ShapeDtypeStruct((R, sb_nbr), BF16), jax.ShapeDtypeStruct((R, kvw), F32),
                   jax.ShapeDtypeStruct((R, kvw), F32), jax.ShapeDtypeStruct((R, sb_nbr), F32)],
        compiler_params=_cparams("arbitrary"), name="sb_in",
    )(h1, g1, w_sb_in)

    kb = SB_KEY_BLOCK
    tri = (lax.broadcasted_iota(jnp.int32, (kb, kb), 0) >= lax.broadcasted_iota(jnp.int32, (kb, kb), 1)).astype(BF16)
    sb_scale = float(hd) ** -0.5
    kv_spec = pl.BlockSpec((LP, hd), lambda b, h: (b, h // HG))
    ogp1 = pl.pallas_call(
        functools.partial(_sb_prompt_attn_kernel, chunks=chunks, pad=pad, scale=sb_scale, kb=kb),
        grid=(B, SH),
        in_specs=[head_spec, kv_spec, kv_spec, head_spec, pl.BlockSpec((kb, kb), lambda b, h: (0, 0))],
        out_specs=head_spec,
        out_shape=jax.ShapeDtypeStruct((RP, sb_nbr), BF16),
        compiler_params=_cparams("arbitrary", "arbitrary"), name="sb_prompt_attn",
    )(sbq, sbk, sbv, gate1, tri)

    rpg = T * HG
    q_s = jnp.transpose(sbq[RP:].reshape(NB, T, G, HG, hd), (0, 2, 1, 3, 4)).reshape(NB, G, rpg, hd)
    knew = jnp.pad(sbk[RP:].reshape(NB, T, kvw), ((0, 0), (0, tp - T), (0, 0)))
    vnew = jnp.pad(sbv[RP:].reshape(NB, T, kvw), ((0, 0), (0, tp - T), (0, 0)))
    ck = cache_sb_k.reshape(cache_sb_k.shape[:2] + (page * G, hd))
    cv = cache_sb_v.reshape(cache_sb_v.shape[:2] + (page * G, hd))
    o_s = pl.pallas_call(
        functools.partial(_sb_sample_attn_kernel, npp=npp, nchunk=nchunk, n_pages=n_pages, groups=G,
                          scale=sb_scale, t_new=T, kb=kb),
        grid_spec=pltpu.PrefetchScalarGridSpec(
            num_scalar_prefetch=1, grid=(NB,),
            in_specs=[pl.BlockSpec((None, G, rpg, hd), lambda s, pt: (s, 0, 0, 0)),
                      seq_spec(tp, kvw), seq_spec(tp, kvw),
                      pl.BlockSpec((kb, kb), lambda s, pt: (0, 0)), any_spec, any_spec],
            out_specs=seq_spec(G * rpg, hd),
            scratch_shapes=[pltpu.VMEM((2, npp, page * G, hd), F32), pltpu.VMEM((2, npp, page * G, hd), F32),
                            pltpu.SemaphoreType.DMA((2, 2)),
                            pltpu.VMEM((G * rpg, LANES), F32), pltpu.VMEM((G * rpg, hd), F32)]),
        out_shape=jax.ShapeDtypeStruct((NB, G * rpg, hd), F32),
        compiler_params=_cparams("arbitrary"), name="sb_sample_attn",
    )(page_table, q_s, knew, vnew, tri, ck, cv)
    o_tok = jnp.transpose(o_s.reshape(NB, G, T, HG, hd), (0, 2, 1, 3, 4)).reshape(RS, sb_nbr)
    ogs1 = pl.pallas_call(
        _gate_kernel,
        grid=(1,),
        in_specs=[pl.BlockSpec((RS, sb_nbr), lambda i: (0, 0)), pl.BlockSpec((RS, sb_nbr), lambda i: (sblk, 0))],
        out_specs=pl.BlockSpec((RS, sb_nbr), lambda i: (0, 0)),
        out_shape=jax.ShapeDtypeStruct((RS, sb_nbr), BF16),
        compiler_params=_cparams("arbitrary"), name="sb_sample_gate",
    )(o_tok, gate1)

    bpl = LP // ROW_ALIGN
    spb = S // ROW_ALIGN
    tf = ROW_ALIGN
    y_prompt = pl.pallas_call(
        _outproj_final_kernel,
        grid=(B, spb),
        in_specs=[pl.BlockSpec((tf, D), lambda b, i: (b * bpl + (bpl - spb) + i, 0)),
                  pl.BlockSpec((tf, sb_nbr), lambda b, i: (b * bpl + (bpl - spb) + i, 0)),
                  _const_spec((sb_nbr, D)), _const_spec((1, D))],
        out_specs=pl.BlockSpec((None, tf, D), lambda b, i: (b, i, 0)),
        out_shape=jax.ShapeDtypeStruct((B, S, D), F32),
        compiler_params=_cparams("arbitrary", "arbitrary"), name="outproj1_prompt",
    )(h1, ogp1, w_out1, gf)
    y_sample = pl.pallas_call(
        _outproj_final_kernel,
        grid=(RS // tf,),
        in_specs=[pl.BlockSpec((tf, D), lambda i: (RP // tf + i, 0)),
                  pl.BlockSpec((tf, sb_nbr), lambda i: (i, 0)),
                  _const_spec((sb_nbr, D)), _const_spec((1, D))],
        out_specs=pl.BlockSpec((tf, D), lambda i: (i, 0)),
        out_shape=jax.ShapeDtypeStruct((RS, D), F32),
        compiler_params=_cparams("arbitrary"), name="outproj1_sample",
    )(h1, ogs1, w_out1, gf).reshape(NB, T, D)

    def prompt_rows(a):
        return a[:RP].reshape((B, LP) + a.shape[1:])[:, pad:][None]

    def sample_rows(a):
        return a[RP:].reshape((NB, T) + a.shape[1:])[None]

    sbk4 = sbk.reshape(R, G, hd)
    sbv4 = sbv.reshape(R, G, hd)
    return (y_prompt, y_sample,
            prompt_rows(c_all), prompt_rows(kr_all), prompt_rows(sbk4), prompt_rows(sbv4),
            sample_rows(c_all), sample_rows(kr_all), sample_rows(sbk4), sample_rows(sbv4))
```

```python
import functools

import jax
import jax.numpy as jnp
from jax import lax
from jax.experimental import pallas as pl
from jax.experimental.pallas import tpu as pltpu

EPS = 1e-6
NEG_INF = -1e30
ROPE_THETA = 10000.0
LANES = 128
BF16_ROWS = 16
ROW_ALIGN = 128
VMEM_LIMIT = 56 * 1024 * 1024
ROW_TILE = 512
COL_TILE = 1024
QUERY_CHUNK = 512
SB_QUERY_CHUNK = 256
PAGES_PER_CHUNK = 16
PAGES_PER_UPDATE = 16
RING_SLOTS = 3
SB_KEY_BLOCK = 256
LOG2_E = 1.4426950408889634
DEAD_MASS_LOG2 = 151.0

F32 = jnp.float32
BF16 = jnp.bfloat16


def _cparams(*sem):
    return pltpu.CompilerParams(dimension_semantics=sem, vmem_limit_bytes=VMEM_LIMIT)


def _rms(x, g):
    return x * lax.rsqrt(jnp.mean(x * x, axis=-1, keepdims=True) + EPS) * g


def _dot(a, b):
    return jnp.dot(a, b, preferred_element_type=F32)


def _dot_nt(a, b):
    return lax.dot_general(a, b, (((1,), (1,)), ((), ())), preferred_element_type=F32)


def _silu(x):
    return x * (1.0 / (1.0 + jnp.exp(-x)))


def _const_spec(shape):
    nd = len(shape)
    return pl.BlockSpec(shape, lambda *_: (0,) * nd, pipeline_mode=pl.Buffered(1))


def _col_tiles(n):
    return [(a, min(a + COL_TILE, n)) for a in range(0, n, COL_TILE)]


def _mla_in_kernel(h_ref, g_ref, w_ref, gq_ref, gkv_ref, cs_ref, sn_ref,
                   cqn_ref, c_ref, kr_ref, krp_ref, gate_ref, *, lq, lkv, nbr, rope):
    xn = _rms(h_ref[...], g_ref[...]).astype(BF16)
    cqn_ref[...] = _rms(_dot(xn, w_ref[:, :lq]), gq_ref[...]).astype(BF16)
    c_ref[...] = _rms(_dot(xn, w_ref[:, lq:lq + lkv]), gkv_ref[...])
    o = lq + lkv
    for a, b in _col_tiles(nbr):
        gate_ref[:, a:b] = _dot(xn, w_ref[:, o + a:o + b])
    o += nbr
    u = _dot(xn, w_ref[:, o:o + 2 * LANES])
    kr = u[:, :LANES] * cs_ref[...] + u[:, LANES:] * sn_ref[...]
    kr_ref[...] = kr[:, :rope]
    krp_ref[...] = kr.astype(BF16)


def _mla_up_kernel(cqn_ref, c_ref, cs_ref, sn_ref, wqn_ref, wqa_ref, wqb_ref, wuk_ref, wuv_ref,
                   qn_ref, qr_ref, kn_ref, v_ref, *, heads):
    cqn = cqn_ref[...]
    c16 = c_ref[...].astype(BF16)
    cs = jnp.concatenate([cs_ref[...]] * (COL_TILE // LANES), axis=1)
    sn = jnp.concatenate([sn_ref[...]] * (COL_TILE // LANES), axis=1)
    for a, b in _col_tiles(heads * LANES):
        qn_ref[:, a:b] = _dot(cqn, wqn_ref[:, a:b]).astype(BF16)
        qr_ref[:, a:b] = (_dot(cqn, wqa_ref[:, a:b]) * cs[:, :b - a]
                          + _dot(cqn, wqb_ref[:, a:b]) * sn[:, :b - a]).astype(BF16)
        kn_ref[:, a:b] = _dot(c16, wuk_ref[:, a:b]).astype(BF16)
        v_ref[:, a:b] = _dot(c16, wuv_ref[:, a:b]).astype(BF16)


def _mla_prompt_attn_kernel(qn_ref, qr_ref, kn_ref, kr_ref, v_ref, gate_ref, o_ref, *, chunks, pad, scale):
    def keys(a, b):
        return jnp.concatenate([kn_ref[a:b, :], kr_ref[a:b, :]], axis=-1)

    for r0, r1 in chunks:
        nq = r1 - r0
        q = jnp.concatenate([qn_ref[r0:r1, :], qr_ref[r0:r1, :]], axis=-1)
        d0 = max(r0, pad)
        s_d = _dot_nt(q, keys(d0, r1)) * scale
        row = r0 + lax.broadcasted_iota(jnp.int32, (nq, r1 - d0), 0)
        col = d0 + lax.broadcasted_iota(jnp.int32, (nq, r1 - d0), 1)
        s_d = jnp.where(col <= row, s_d, NEG_INF)
        m = jnp.max(s_d, axis=-1, keepdims=True)
        if d0 > pad:
            s_f = _dot_nt(q, keys(pad, d0)) * scale
            m = jnp.maximum(m, jnp.max(s_f, axis=-1, keepdims=True))
        p_d = jnp.exp(s_d - m)
        l = jnp.sum(p_d, axis=-1, keepdims=True)
        o = _dot(p_d.astype(BF16), v_ref[d0:r1, :])
        if d0 > pad:
            p_f = jnp.exp(s_f - m)
            l = l + jnp.sum(p_f, axis=-1, keepdims=True)
            o = o + _dot(p_f.astype(BF16), v_ref[pad:d0, :])
        o_ref[r0:r1, :] = (o * (1.0 / l) * _silu(gate_ref[r0:r1, :])).astype(BF16)


def _page_copies(pt_ref, hbm_refs, bufs, sems, seq, first_page, slot, npp, lookup):
    out = []
    for kk in range(npp):
        pid = pt_ref[seq, first_page + kk] if lookup else 0
        for a, (hbm, buf) in enumerate(zip(hbm_refs, bufs)):
            out.append(pltpu.make_async_copy(hbm.at[0, pid], buf.at[slot, kk], sems.at[a, slot]))
    return out


def _start_pages(pt_ref, hbm_refs, bufs, sems, seq, first_page, slot, npp):
    for i, cp in enumerate(_page_copies(pt_ref, hbm_refs, bufs, sems, seq, first_page, slot, npp, True)):
        cp.start(priority=i % 2)


def _wait_pages(pt_ref, hbm_refs, bufs, sems, slot, npp):
    for cp in _page_copies(pt_ref, hbm_refs, bufs, sems, 0, 0, slot, npp, False):
        cp.wait()


def _ring_loop(pt_ref, hbm_refs, bufs, sems, seq, nseq, n, npp, first_page_of, compute, *, across_sequences):
    slots = bufs[0].shape[0]
    ahead = slots - 1

    def start(sq, chunk, slot):
        _start_pages(pt_ref, hbm_refs, bufs, sems, sq, first_page_of(chunk), slot, npp)

    assert n >= ahead or not across_sequences
    base = seq * n if across_sequences else 0
    for i in range(min(ahead, n)):
        if across_sequences:
            @pl.when(seq == 0)
            def _(i=i):
                start(seq, i, i)
        else:
            start(seq, i, i)

    def body(c, carry):
        t = base + c
        slot = lax.rem(t, slots)
        c2 = c + ahead
        slot2 = lax.rem(t + ahead, slots)

        @pl.when(c2 < n)
        def _():
            start(seq, c2, slot2)

        if across_sequences:
            @pl.when((c2 >= n) & (seq + 1 < nseq))
            def _():
                start(seq + 1, c2 - n, slot2)

        _wait_pages(pt_ref, hbm_refs, bufs, sems, slot, npp)
        compute(slot)
        return carry

    lax.fori_loop(0, n, body, 0)


def _mla_qlat_kernel(qn_ref, wukt_ref, o_ref):
    o_ref[...] = _dot(qn_ref[...], wukt_ref[...]).astype(BF16)


def _token_of_row(row, rows_per_token, n_tokens):
    t = jnp.zeros(row.shape, jnp.int32)
    for i in range(1, n_tokens):
        t = t + (row >= i * rows_per_token).astype(jnp.int32)
    return t


def _mla_sample_attn_kernel(pt_ref, ql_ref, qr_ref, cnew_ref, krnewt_ref, lat_hbm, krt_hbm, o_ref,
                            lat_buf, krt_buf, sems, m_ref, l_ref, acc_ref,
                            *, npp, nchunk, sub, t_new, heads, scale):
    ql = ql_ref[...]
    qr = qr_ref[...]
    nrow = ql.shape[0]
    page = lat_buf.shape[2]
    lkv = lat_buf.shape[3]

    def update(c16, krt16, mask):
        s = (_dot_nt(ql, c16) + _dot(qr, krt16)) * scale
        if mask is not None:
            s = jnp.where(mask, s, NEG_INF)
        m_old = m_ref[...]
        m_new = jnp.maximum(m_old, jnp.max(s, axis=-1, keepdims=True))
        alpha = jnp.exp(m_old - m_new)
        p = jnp.exp(s - m_new)
        l_ref[...] = l_ref[...] * alpha + jnp.sum(p, axis=-1, keepdims=True)
        acc_ref[...] = acc_ref[...] * alpha + _dot(p.astype(BF16), c16)
        m_ref[...] = m_new

    m_ref[...] = jnp.full(m_ref.shape, NEG_INF, F32)
    l_ref[...] = jnp.zeros(l_ref.shape, F32)
    acc_ref[...] = jnp.zeros(acc_ref.shape, F32)
    nn = cnew_ref.shape[0]
    cnew16 = jnp.concatenate([cnew_ref[...], jnp.zeros((page - nn, lkv), F32)], axis=0).astype(BF16)
    tq = _token_of_row(lax.broadcasted_iota(jnp.int32, (nrow, page), 0), heads, t_new)
    col = lax.broadcasted_iota(jnp.int32, (nrow, page), 1)
    update(cnew16, krnewt_ref[...].astype(BF16), col <= tq)

    def compute(slot):
        for i in range(npp // sub):
            c16 = lat_buf[slot, i * sub:(i + 1) * sub].reshape(sub * page, lkv).astype(BF16)
            krt16 = jnp.concatenate([krt_buf[slot, i * sub + k].astype(BF16) for k in range(sub)], axis=1)
            update(c16, krt16, None)

    _ring_loop(pt_ref, (lat_hbm, krt_hbm), (lat_buf, krt_buf), sems, pl.program_id(0), pl.num_programs(0),
               nchunk, npp, lambda c: c * npp, compute, across_sequences=True)
    o_ref[...] = (acc_ref[...] * (1.0 / l_ref[...])).astype(BF16)


def _mla_sample_out_kernel(ol_ref, wuv_ref, gate_ref, o_ref):
    o = _dot(ol_ref[...], wuv_ref[...])
    o_ref[...] = (o * _silu(gate_ref[...])).astype(BF16)


def _outproj_kernel(h_ref, ogp_ref, ogs_ref, w_ref, o_ref, *, n_prompt_blocks):
    i = pl.program_id(0)

    @pl.when(i < n_prompt_blocks)
    def _():
        o_ref[...] = h_ref[...] + _dot(ogp_ref[...], w_ref[...])

    @pl.when(i >= n_prompt_blocks)
    def _():
        o_ref[...] = h_ref[...] + _dot(ogs_ref[...], w_ref[...])


def _sb_in_kernel(h_ref, g_ref, w_ref, q_ref, k_ref, v_ref, gate_ref, *, nbr, kvw):
    xn = _rms(h_ref[...], g_ref[...]).astype(BF16)
    for a, b in _col_tiles(nbr):
        q_ref[:, a:b] = _dot(xn, w_ref[:, a:b]).astype(BF16)
    k_ref[...] = _dot(xn, w_ref[:, nbr:nbr + kvw])
    v_ref[...] = _dot(xn, w_ref[:, nbr + kvw:nbr + 2 * kvw])
    o = nbr + 2 * kvw
    for a, b in _col_tiles(nbr):
        gate_ref[:, a:b] = _dot(xn, w_ref[:, o + a:o + b])


def _outproj_final_kernel(h_ref, og_ref, w_ref, g_ref, y_ref):
    h = h_ref[...] + _dot(og_ref[...], w_ref[...])
    y_ref[...] = _rms(h, g_ref[...])


def _drop_log2(u):
    return jnp.maximum(u, 0.0) + jnp.log2(1.0 + jnp.exp2(-jnp.abs(u)))


def _suffix_sums(x, tri):
    hi = x.astype(BF16)
    lo = (x - hi.astype(F32)).astype(BF16)
    return _dot(hi, tri) + _dot(lo, tri)


def _sb_span(u, v16, tri, masks, used, kb):
    nq, nk = u.shape
    nblk = nk // kb
    drop = _drop_log2(u)
    blocks = []
    for b in range(nblk):
        blk = drop[:, b * kb:(b + 1) * kb]
        blocks.append(blk if masks[b] is None else jnp.where(masks[b], blk, 0.0))
    ssum = _suffix_sums(jnp.concatenate(blocks, axis=0), tri)
    a_blocks = [None] * nblk
    for b in reversed(range(nblk)):
        sb = ssum[b * nq:(b + 1) * nq, :]
        a = jnp.exp2(u[:, b * kb:(b + 1) * kb] - sb - used)
        if masks[b] is not None:
            a = jnp.where(masks[b], a, 0.0)
        a_blocks[b] = a.astype(BF16)
        used = used + sb[:, 0:1]
    return _dot(jnp.concatenate(a_blocks, axis=1), v16), used


def _sb_prompt_attn_kernel(q_ref, k_ref, v_ref, gate_ref, tri_ref, o_ref, acc_ref, used_ref,
                           *, chunks, pad, scale2, kb):
    hd = q_ref.shape[1]
    tri = tri_ref[...]
    fb = LANES
    far = []
    for r0, r1 in chunks:
        nq = r1 - r0
        nblk = min(-(-(r1 - r0 + kb) // kb), -(-r1 // kb))
        w0 = r1 - nblk * kb
        ext = max(-w0, 0)
        w0 = max(w0, 0)
        k16 = k_ref[w0:r1, :].astype(BF16)
        v16 = v_ref[w0:r1, :].astype(BF16)
        if ext:
            k16 = jnp.concatenate([jnp.zeros((ext, hd), BF16), k16], axis=0)
            v16 = jnp.concatenate([jnp.zeros((ext, hd), BF16), v16], axis=0)
        masks = []
        for b in range(nblk):
            c0 = w0 - ext + b * kb
            if c0 + kb <= r0 and c0 >= pad:
                masks.append(None)
            else:
                row = r0 + lax.broadcasted_iota(jnp.int32, (nq, kb), 0)
                col = c0 + lax.broadcasted_iota(jnp.int32, (nq, kb), 1)
                masks.append((col < row) & (col >= pad))
        o, used = _sb_span(_dot_nt(q_ref[r0:r1, :], k16) * scale2, v16, tri, masks, jnp.zeros((nq, 1), F32), kb)
        acc_ref[r0:r1, :] = o
        used_ref[r0:r1, :] = jnp.broadcast_to(used, (nq, LANES))
        if w0 > 0:
            far.append((r0, r1, w0))

    for r0, r1, w0 in far:
        nq = r1 - r0

        @pl.when(jnp.min(used_ref[r0:r1, :]) < DEAD_MASS_LOG2)
        def _(r0=r0, r1=r1, w0=w0, nq=nq):
            q = q_ref[r0:r1, :]

            def far_block(i, carry):
                c0 = pl.multiple_of(w0 - (i + 1) * fb, fb)
                kf = k_ref[pl.ds(c0, fb), :].astype(BF16)
                vf = v_ref[pl.ds(c0, fb), :].astype(BF16)
                col = c0 + lax.broadcasted_iota(jnp.int32, (nq, fb), 1)
                of, uf = _sb_span(_dot_nt(q, kf) * scale2, vf, tri_ref[0:fb, 0:fb], [col >= pad],
                                  used_ref[r0:r1, 0:1], fb)
                acc_ref[r0:r1, :] += of
                used_ref[r0:r1, :] = jnp.broadcast_to(uf, (nq, LANES))
                return carry

            lax.fori_loop(0, w0 // fb, far_block, 0)

    o_ref[...] = (acc_ref[...] * _silu(gate_ref[...])).astype(BF16)


def _sb_sample_attn_kernel(pt_ref, q_ref, knew_ref, vnew_ref, tri_ref, k_hbm, v_hbm, o_ref,
                           k0_buf, v0_buf, sems0, k_buf, v_buf, sems, used_ref, acc_ref,
                           *, npp, nchunk, n_pages, groups, scale2, t_new, kb):
    seq = pl.program_id(0)
    nseq = pl.num_programs(0)
    rpg = q_ref.shape[1]
    hd = q_ref.shape[2]
    page = k_buf.shape[2] // groups
    nrow = groups * rpg
    qs = [q_ref[g] for g in range(groups)]
    first_page_of = lambda c: n_pages - (c + 1) * npp

    slot0 = lax.rem(seq, 2)

    @pl.when(seq == 0)
    def _():
        _start_pages(pt_ref, (k_hbm, v_hbm), (k0_buf, v0_buf), sems0, seq, first_page_of(0), 0, npp)

    @pl.when(seq + 1 < nseq)
    def _():
        _start_pages(pt_ref, (k_hbm, v_hbm), (k0_buf, v0_buf), sems0, seq + 1, first_page_of(0), 1 - slot0, npp)

    def span(kg, vg, tri, masks, width):
        u = jnp.concatenate([_dot_nt(qs[g], kg[g]) for g in range(groups)], axis=0) * scale2
        nblk = u.shape[1] // width
        drop = _drop_log2(u)
        blocks = []
        for b in range(nblk):
            blk = drop[:, b * width:(b + 1) * width]
            blocks.append(blk if masks is None else jnp.where(masks[b], blk, 0.0))
        ssum = _suffix_sums(jnp.concatenate(blocks, axis=0), tri)
        used = used_ref[:, 0:1]
        a_blocks = [None] * nblk
        for b in reversed(range(nblk)):
            sb = ssum[b * nrow:(b + 1) * nrow, :]
            a = jnp.exp2(u[:, b * width:(b + 1) * width] - sb - used)
            if masks is not None:
                a = jnp.where(masks[b], a, 0.0)
            a_blocks[b] = a.astype(BF16)
            used = used + sb[:, 0:1]
        a16 = jnp.concatenate(a_blocks, axis=1)
        acc_ref[...] += jnp.concatenate([_dot(a16[g * rpg:(g + 1) * rpg, :], vg[g]) for g in range(groups)], axis=0)
        used_ref[...] = jnp.broadcast_to(used, used_ref.shape)

    acc_ref[...] = jnp.zeros(acc_ref.shape, F32)
    used_ref[...] = jnp.zeros(used_ref.shape, F32)
    nn = knew_ref.shape[0]
    zpad = jnp.zeros((page - nn, hd), F32)
    kg = [jnp.concatenate([knew_ref[:, g * hd:(g + 1) * hd], zpad], axis=0).astype(BF16) for g in range(groups)]
    vg = [jnp.concatenate([vnew_ref[:, g * hd:(g + 1) * hd], zpad], axis=0).astype(BF16) for g in range(groups)]
    tq1 = _token_of_row(lax.broadcasted_iota(jnp.int32, (rpg, page), 0), rpg // t_new, t_new)
    tq = jnp.concatenate([tq1] * groups, axis=0)
    span(kg, vg, tri_ref[0:page, 0:page], [lax.broadcasted_iota(jnp.int32, (nrow, page), 1) < tq], page)

    def chunk(kb_ref, vb_ref, slot):
        kg = [jnp.concatenate([kb_ref[slot, kk, pl.ds(g, page, stride=groups), :] for kk in range(npp)],
                              axis=0).astype(BF16) for g in range(groups)]
        vg = [jnp.concatenate([vb_ref[slot, kk, pl.ds(g, page, stride=groups), :] for kk in range(npp)],
                              axis=0).astype(BF16) for g in range(groups)]
        span(kg, vg, tri_ref[...], None, kb)

    _wait_pages(pt_ref, (k_hbm, v_hbm), (k0_buf, v0_buf), sems0, slot0, npp)
    chunk(k0_buf, v0_buf, slot0)

    @pl.when(jnp.min(used_ref[...]) < DEAD_MASS_LOG2)
    def _():
        _ring_loop(pt_ref, (k_hbm, v_hbm), (k_buf, v_buf), sems, seq, nseq, nchunk - 1, npp,
                   lambda c: first_page_of(c + 1), functools.partial(chunk, k_buf, v_buf),
                   across_sequences=False)

    o_ref[...] = acc_ref[...]


def _gate_kernel(o_ref, gate_ref, og_ref):
    og_ref[...] = (o_ref[...] * _silu(gate_ref[...])).astype(BF16)


def _query_chunks(lp, size):
    n = max(lp // size, 1)
    bounds = [i * size for i in range(n)] + [lp]
    return tuple((bounds[i], bounds[i + 1]) for i in range(n))


def _sb_query_chunks(lp, size):
    return tuple((a, min(a + size, lp)) for a in range(0, lp, size))


def kernel(x_prompt, x_sample, cache_mla_latent, cache_mla_krope, cache_sb_k, cache_sb_v, page_table,
           meta_tokens, mla_norm, mla_w_in, mla_q_norm, mla_kv_norm, mla_w_uq, mla_w_uk, mla_w_uv, mla_w_out,
           sb_norm, sb_w_in, sb_w_out, final_norm):
    B, S, D = x_prompt.shape
    NB, T = x_sample.shape[:2]
    n_meta = meta_tokens.shape[0]
    n_pages = page_table.shape[1]
    page = cache_mla_latent.shape[2]
    lq = mla_q_norm.shape[-1]
    lkv = mla_kv_norm.shape[-1]
    H = mla_w_uq.shape[2]
    nope = mla_w_uk.shape[-1]
    rope = mla_w_uq.shape[-1] - nope
    vd = mla_w_uv.shape[-1]
    nbr = H * vd
    half = rope // 2
    G = cache_sb_k.shape[3]
    hd = cache_sb_k.shape[4]
    kvw = G * hd
    sb_nbr = sb_w_out.shape[1]
    SH = sb_nbr // hd
    HG = SH // G
    npp = PAGES_PER_CHUNK
    assert nope == LANES and vd == LANES and hd == LANES and page == LANES and 2 * rope == LANES
    assert mla_norm.shape[0] == 1 and sb_norm.shape[0] == 1, "one layer of each mixer"
    assert n_pages % npp == 0 and npp % PAGES_PER_UPDATE == 0 and (npp * page) % SB_KEY_BLOCK == 0 and T <= 8

    L = n_meta + S
    pad = (-L) % ROW_ALIGN
    LP = L + pad
    RP = B * LP
    RS = NB * T
    R = RP + RS
    tm = ROW_TILE
    assert RP % tm == 0 and RS % tm == 0 and RP % RS == 0 and S % ROW_ALIGN == 0 and pad % BF16_ROWS == 0
    nblk = R // tm
    past = n_pages * page

    hp = jnp.concatenate([jnp.zeros((B, pad, D), F32),
                          jnp.broadcast_to(meta_tokens.astype(F32)[None], (B, n_meta, D)), x_prompt], axis=1)
    h0 = jnp.concatenate([hp.reshape(RP, D), x_sample.reshape(RS, D)], axis=0)

    pos = jnp.concatenate([jnp.tile(jnp.arange(LP, dtype=jnp.int32) - pad, B),
                           jnp.tile(past + jnp.arange(T, dtype=jnp.int32), NB)])
    inv = ROPE_THETA ** (-jnp.arange(half, dtype=F32) / half)
    ang = pos.astype(F32)[:, None] * inv[None, :]
    zpadl = jnp.zeros((R, LANES - rope), F32)
    cs = jnp.concatenate([jnp.cos(ang), jnp.cos(ang), zpadl], axis=1)
    sn = jnp.concatenate([jnp.sin(ang), jnp.sin(ang), zpadl], axis=1)

    def rot_cols(w):
        return jnp.concatenate([-w[..., half:], w[..., :half]], axis=-1)

    def lane_pad(w):
        return jnp.concatenate([w, jnp.zeros(w.shape[:-1] + (LANES - w.shape[-1],), w.dtype)], axis=-1)

    w_in = mla_w_in[0]
    o_kr = lq + lkv
    w_kr = w_in[:, o_kr:o_kr + rope]
    w_in_ext = jnp.concatenate([w_in[:, :o_kr], w_in[:, o_kr + rope:], lane_pad(w_kr), lane_pad(rot_cols(w_kr))],
                               axis=1).astype(BF16)
    n_in = w_in_ext.shape[1]
    w_uq = mla_w_uq[0]
    w_qn = w_uq[:, :, :nope].reshape(lq, H * nope).astype(BF16)
    w_qa = lane_pad(w_uq[:, :, nope:]).reshape(lq, H * LANES).astype(BF16)
    w_qb = lane_pad(rot_cols(w_uq[:, :, nope:])).reshape(lq, H * LANES).astype(BF16)
    w_uk = mla_w_uk[0].reshape(lkv, H * nope).astype(BF16)
    w_uv = mla_w_uv[0].reshape(lkv, H * vd).astype(BF16)
    w_ukt = jnp.transpose(mla_w_uk[0], (1, 2, 0)).astype(BF16)
    w_out0 = mla_w_out[0].astype(BF16)
    w_sb_in = sb_w_in[0].astype(BF16)
    w_out1 = sb_w_out[0].astype(BF16)
    g0 = mla_norm[0].reshape(1, D)
    gq = mla_q_norm[0].reshape(1, lq)
    gkv = mla_kv_norm[0].reshape(1, lkv)
    g1 = sb_norm[0].reshape(1, D)
    gf = final_norm.reshape(1, D)

    row_spec = lambda w: pl.BlockSpec((tm, w), lambda i: (i, 0))

    cqn, c_all, kr_all, krp_all, gate0 = pl.pallas_call(
        functools.partial(_mla_in_kernel, lq=lq, lkv=lkv, nbr=nbr, rope=rope),
        grid=(nblk,),
        in_specs=[row_spec(D), _const_spec((1, D)), _const_spec((D, n_in)), _const_spec((1, lq)),
                  _const_spec((1, lkv)), row_spec(LANES), row_spec(LANES)],
        out_specs=[row_spec(lq), row_spec(lkv), row_spec(rope), row_spec(LANES), row_spec(nbr)],
        out_shape=[jax.ShapeDtypeStruct((R, lq), BF16), jax.ShapeDtypeStruct((R, lkv), F32),
                   jax.ShapeDtypeStruct((R, rope), F32), jax.ShapeDtypeStruct((R, LANES), BF16),
                   jax.ShapeDtypeStruct((R, nbr), F32)],
        compiler_params=_cparams("arbitrary"), name="mla_in",
    )(h0, g0, w_in_ext, gq, gkv, cs, sn)

    qn_all, qr_all, kn_all, v_all = pl.pallas_call(
        functools.partial(_mla_up_kernel, heads=H),
        grid=(nblk,),
        in_specs=[row_spec(lq), row_spec(lkv), row_spec(LANES), row_spec(LANES),
                  _const_spec((lq, H * nope)), _const_spec((lq, H * LANES)), _const_spec((lq, H * LANES)),
                  _const_spec((lkv, H * nope)), _const_spec((lkv, H * vd))],
        out_specs=[row_spec(H * nope), row_spec(H * LANES), row_spec(H * nope), row_spec(H * vd)],
        out_shape=[jax.ShapeDtypeStruct((R, H * nope), BF16), jax.ShapeDtypeStruct((R, H * LANES), BF16),
                   jax.ShapeDtypeStruct((R, H * nope), BF16), jax.ShapeDtypeStruct((R, H * vd), BF16)],
        compiler_params=_cparams("arbitrary"), name="mla_up",
    )(cqn, c_all, cs, sn, w_qn, w_qa, w_qb, w_uk, w_uv)

    chunks = _query_chunks(LP, QUERY_CHUNK)
    mla_scale = float(nope + rope) ** -0.5
    head_spec = pl.BlockSpec((LP, LANES), lambda b, h: (b, h))
    ogp0 = pl.pallas_call(
        functools.partial(_mla_prompt_attn_kernel, chunks=chunks, pad=pad, scale=mla_scale),
        grid=(B, H),
        in_specs=[head_spec, head_spec, head_spec, pl.BlockSpec((LP, LANES), lambda b, h: (b, 0)),
                  head_spec, head_spec],
        out_specs=head_spec,
        out_shape=jax.ShapeDtypeStruct((RP, nbr), BF16),
        compiler_params=_cparams("arbitrary", "arbitrary"), name="mla_prompt_attn",
    )(qn_all, qr_all, kn_all, krp_all, v_all, gate0)

    sblk = RP // RS
    qlat = pl.pallas_call(
        _mla_qlat_kernel,
        grid=(H,),
        in_specs=[pl.BlockSpec((RS, nope), lambda h: (sblk, h)),
                  pl.BlockSpec((None, nope, lkv), lambda h: (h, 0, 0))],
        out_specs=pl.BlockSpec((None, RS, lkv), lambda h: (h, 0, 0)),
        out_shape=jax.ShapeDtypeStruct((H, RS, lkv), BF16),
        compiler_params=_cparams("arbitrary"), name="mla_qlat",
    )(qn_all, w_ukt)
    qlat_s = jnp.transpose(qlat.reshape(H, NB, T, lkv), (1, 2, 0, 3)).reshape(NB, T * H, lkv)
    qrope_s = qr_all[RP:].reshape(NB, T, H, LANES)[..., :rope].reshape(NB, T * H, rope)
    tp = 8
    cnew = jnp.pad(c_all[RP:].reshape(NB, T, lkv), ((0, 0), (0, tp - T), (0, 0)))
    krnew_t = jnp.pad(jnp.swapaxes(kr_all[RP:].reshape(NB, T, rope), 1, 2), ((0, 0), (0, 0), (0, page - T)))
    krope_t = jnp.swapaxes(cache_mla_krope, 2, 3)

    nchunk = n_pages // npp
    nrow = T * H
    seq_spec = lambda r, w: pl.BlockSpec((None, r, w), lambda s, pt: (s, 0, 0))
    any_spec = pl.BlockSpec(memory_space=pl.ANY)
    olat = pl.pallas_call(
        functools.partial(_mla_sample_attn_kernel, npp=npp, nchunk=nchunk, sub=PAGES_PER_UPDATE,
                          t_new=T, heads=H, scale=mla_scale),
        grid_spec=pltpu.PrefetchScalarGridSpec(
            num_scalar_prefetch=1, grid=(NB,),
            in_specs=[seq_spec(nrow, lkv), seq_spec(nrow, rope), seq_spec(tp, lkv), seq_spec(rope, page),
                      any_spec, any_spec],
            out_specs=seq_spec(nrow, lkv),
            scratch_shapes=[pltpu.VMEM((RING_SLOTS, npp, page, lkv), F32),
                            pltpu.VMEM((RING_SLOTS, npp, rope, page), F32),
                            pltpu.SemaphoreType.DMA((2, RING_SLOTS)),
                            pltpu.VMEM((nrow, 1), F32), pltpu.VMEM((nrow, 1), F32), pltpu.VMEM((nrow, lkv), F32)]),
        out_shape=jax.ShapeDtypeStruct((NB, nrow, lkv), BF16),
        compiler_params=_cparams("arbitrary"), name="mla_sample_attn",
    )(page_table, qlat_s, qrope_s, cnew, krnew_t, cache_mla_latent, krope_t)
    olat_h = jnp.transpose(olat.reshape(NB, T, H, lkv), (2, 0, 1, 3)).reshape(H, RS, lkv)
    ogs0 = pl.pallas_call(
        _mla_sample_out_kernel,
        grid=(H,),
        in_specs=[pl.BlockSpec((None, RS, lkv), lambda h: (h, 0, 0)),
                  pl.BlockSpec((lkv, vd), lambda h: (0, h)),
                  pl.BlockSpec((RS, vd), lambda h: (sblk, h))],
        out_specs=pl.BlockSpec((RS, vd), lambda h: (0, h)),
        out_shape=jax.ShapeDtypeStruct((RS, nbr), BF16),
        compiler_params=_cparams("arbitrary"), name="mla_sample_out",
    )(olat_h, w_uv, gate0)

    npb = RP // tm
    h1 = pl.pallas_call(
        functools.partial(_outproj_kernel, n_prompt_blocks=npb),
        grid=(nblk,),
        in_specs=[row_spec(D),
                  pl.BlockSpec((tm, nbr), lambda i: (jnp.minimum(i, npb - 1), 0)),
                  pl.BlockSpec((tm, nbr), lambda i: (jnp.maximum(i - npb, 0), 0)),
                  _const_spec((nbr, D))],
        out_specs=row_spec(D),
        out_shape=jax.ShapeDtypeStruct((R, D), F32),
        compiler_params=_cparams("arbitrary"), name="outproj0",
    )(h0, ogp0, ogs0, w_out0)

    sbq, sbk, sbv, gate1 = pl.pallas_call(
        functools.partial(_sb_in_kernel, nbr=sb_nbr, kvw=kvw),
        grid=(nblk,),
        in_specs=[row_spec(D), _const_spec((1, D)), _const_spec((D, 2 * sb_nbr + 2 * kvw))],
        out_specs=[row_spec(sb_nbr), row_spec(kvw), row_spec(kvw), row_spec(sb_nbr)],
        out_shape=[jax.ShapeDtypeStruct((R, sb_nbr), BF16), jax.ShapeDtypeStruct((R, kvw), F32),
                   jax.ShapeDtypeStruct((R, kvw), F32), jax.ShapeDtypeStruct((R, sb_nbr), F32)],
        compiler_params=_cparams("arbitrary"), name="sb_in",
    )(h1, g1, w_sb_in)

    kb = SB_KEY_BLOCK
    tri = (lax.broadcasted_iota(jnp.int32, (kb, kb), 0) >= lax.broadcasted_iota(jnp.int32, (kb, kb), 1)).astype(BF16)
    sb_scale2 = float(hd) ** -0.5 * LOG2_E
    kv_spec = pl.BlockSpec((LP, hd), lambda b, h: (b, h // HG))
    ogp1 = pl.pallas_call(
        functools.partial(_sb_prompt_attn_kernel, chunks=_sb_query_chunks(LP, SB_QUERY_CHUNK), pad=pad,
                          scale2=sb_scale2, kb=kb),
        grid=(B, SH),
        in_specs=[head_spec, kv_spec, kv_spec, head_spec, pl.BlockSpec((kb, kb), lambda b, h: (0, 0))],
        out_specs=head_spec,
        scratch_shapes=[pltpu.VMEM((LP, hd), F32), pltpu.VMEM((LP, LANES), F32)],
        out_shape=jax.ShapeDtypeStruct((RP, sb_nbr), BF16),
        compiler_params=_cparams("arbitrary", "arbitrary"), name="sb_prompt_attn",
    )(sbq, sbk, sbv, gate1, tri)

    rpg = T * HG
    q_s = jnp.transpose(sbq[RP:].reshape(NB, T, G, HG, hd), (0, 2, 1, 3, 4)).reshape(NB, G, rpg, hd)
    knew = jnp.pad(sbk[RP:].reshape(NB, T, kvw), ((0, 0), (0, tp - T), (0, 0)))
    vnew = jnp.pad(sbv[RP:].reshape(NB, T, kvw), ((0, 0), (0, tp - T), (0, 0)))
    ck = cache_sb_k.reshape(cache_sb_k.shape[:2] + (page * G, hd))
    cv = cache_sb_v.reshape(cache_sb_v.shape[:2] + (page * G, hd))
    o_s = pl.pallas_call(
        functools.partial(_sb_sample_attn_kernel, npp=npp, nchunk=nchunk, n_pages=n_pages, groups=G,
                          scale2=sb_scale2, t_new=T, kb=kb),
        grid_spec=pltpu.PrefetchScalarGridSpec(
            num_scalar_prefetch=1, grid=(NB,),
            in_specs=[pl.BlockSpec((None, G, rpg, hd), lambda s, pt: (s, 0, 0, 0)),
                      seq_spec(tp, kvw), seq_spec(tp, kvw),
                      pl.BlockSpec((kb, kb), lambda s, pt: (0, 0)), any_spec, any_spec],
            out_specs=seq_spec(G * rpg, hd),
            scratch_shapes=[pltpu.VMEM((2, npp, page * G, hd), F32), pltpu.VMEM((2, npp, page * G, hd), F32),
                            pltpu.SemaphoreType.DMA((2, 2)),
                            pltpu.VMEM((RING_SLOTS, npp, page * G, hd), F32),
                            pltpu.VMEM((RING_SLOTS, npp, page * G, hd), F32),
                            pltpu.SemaphoreType.DMA((2, RING_SLOTS)),
                            pltpu.VMEM((G * rpg, LANES), F32), pltpu.VMEM((G * rpg, hd), F32)]),
        out_shape=jax.ShapeDtypeStruct((NB, G * rpg, hd), F32),
        compiler_params=_cparams("arbitrary"), name="sb_sample_attn",
    )(page_table, q_s, knew, vnew, tri, ck, cv)
    o_tok = jnp.transpose(o_s.reshape(NB, G, T, HG, hd), (0, 2, 1, 3, 4)).reshape(RS, sb_nbr)
    ogs1 = pl.pallas_call(
        _gate_kernel,
        grid=(1,),
        in_specs=[pl.BlockSpec((RS, sb_nbr), lambda i: (0, 0)), pl.BlockSpec((RS, sb_nbr), lambda i: (sblk, 0))],
        out_specs=pl.BlockSpec((RS, sb_nbr), lambda i: (0, 0)),
        out_shape=jax.ShapeDtypeStruct((RS, sb_nbr), BF16),
        compiler_params=_cparams("arbitrary"), name="sb_sample_gate",
    )(o_tok, gate1)

    bpl = LP // ROW_ALIGN
    spb = S // ROW_ALIGN
    tf = ROW_ALIGN
    y_prompt = pl.pallas_call(
        _outproj_final_kernel,
        grid=(B, spb),
        in_specs=[pl.BlockSpec((tf, D), lambda b, i: (b * bpl + (bpl - spb) + i, 0)),
                  pl.BlockSpec((tf, sb_nbr), lambda b, i: (b * bpl + (bpl - spb) + i, 0)),
                  _const_spec((sb_nbr, D)), _const_spec((1, D))],
        out_specs=pl.BlockSpec((None, tf, D), lambda b, i: (b, i, 0)),
        out_shape=jax.ShapeDtypeStruct((B, S, D), F32),
        compiler_params=_cparams("arbitrary", "arbitrary"), name="outproj1_prompt",
    )(h1, ogp1, w_out1, gf)
    y_sample = pl.pallas_call(
        _outproj_final_kernel,
        grid=(RS // tf,),
        in_specs=[pl.BlockSpec((tf, D), lambda i: (RP // tf + i, 0)),
                  pl.BlockSpec((tf, sb_nbr), lambda i: (i, 0)),
                  _const_spec((sb_nbr, D)), _const_spec((1, D))],
        out_specs=pl.BlockSpec((tf, D), lambda i: (i, 0)),
        out_shape=jax.ShapeDtypeStruct((RS, D), F32),
        compiler_params=_cparams("arbitrary"), name="outproj1_sample",
    )(h1, ogs1, w_out1, gf).reshape(NB, T, D)

    def prompt_rows(a):
        return a[:RP].reshape((B, LP) + a.shape[1:])[:, pad:][None]

    def sample_rows(a):
        return a[RP:].reshape((NB, T) + a.shape[1:])[None]

    sbk4 = sbk.reshape(R, G, hd)
    sbv4 = sbv.reshape(R, G, hd)
    return (y_prompt, y_sample,
            prompt_rows(c_all), prompt_rows(kr_all), prompt_rows(sbk4), prompt_rows(sbv4),
            sample_rows(c_all), sample_rows(kr_all), sample_rows(sbk4), sample_rows(sbv4))
```

```python
import functools

import jax
import jax.numpy as jnp
from jax import lax
from jax.experimental import pallas as pl
from jax.experimental.pallas import tpu as pltpu

EPS = 1e-6
NEG_INF = -1e30
ROPE_THETA = 10000.0
LANES = 128
BF16_ROWS = 16
ROW_ALIGN = 128
VMEM_LIMIT = 56 * 1024 * 1024
ROW_TILE = 512
COL_TILE = 1024
QUERY_CHUNK = 512
SB_QUERY_CHUNK = 256
PAGES_PER_CHUNK = 16
SOFTMAX_STREAMS = 4
RING_SLOTS = 3
SB_KEY_BLOCK = 256
LOG2_E = 1.4426950408889634
DEAD_MASS_LOG2 = 151.0

F32 = jnp.float32
BF16 = jnp.bfloat16


def _cparams(*sem):
    return pltpu.CompilerParams(dimension_semantics=sem, vmem_limit_bytes=VMEM_LIMIT)


def _rms(x, g):
    return x * lax.rsqrt(jnp.mean(x * x, axis=-1, keepdims=True) + EPS) * g


def _dot(a, b):
    return jnp.dot(a, b, preferred_element_type=F32)


def _dot_nt(a, b):
    return lax.dot_general(a, b, (((1,), (1,)), ((), ())), preferred_element_type=F32)


def _silu(x):
    return x * (1.0 / (1.0 + jnp.exp(-x)))


def _const_spec(shape):
    nd = len(shape)
    return pl.BlockSpec(shape, lambda *_: (0,) * nd, pipeline_mode=pl.Buffered(1))


def _col_tiles(n):
    return [(a, min(a + COL_TILE, n)) for a in range(0, n, COL_TILE)]


def _mla_in_kernel(h_ref, g_ref, w_ref, gq_ref, gkv_ref, cs_ref, sn_ref,
                   cqn_ref, c_ref, kr_ref, krp_ref, gate_ref, *, lq, lkv, nbr, rope):
    xn = _rms(h_ref[...], g_ref[...]).astype(BF16)
    cqn_ref[...] = _rms(_dot(xn, w_ref[:, :lq]), gq_ref[...]).astype(BF16)
    c_ref[...] = _rms(_dot(xn, w_ref[:, lq:lq + lkv]), gkv_ref[...])
    o = lq + lkv
    for a, b in _col_tiles(nbr):
        gate_ref[:, a:b] = _dot(xn, w_ref[:, o + a:o + b])
    o += nbr
    u = _dot(xn, w_ref[:, o:o + 2 * LANES])
    kr = u[:, :LANES] * cs_ref[...] + u[:, LANES:] * sn_ref[...]
    kr_ref[...] = kr[:, :rope]
    krp_ref[...] = kr.astype(BF16)


def _mla_up_kernel(cqn_ref, c_ref, cs_ref, sn_ref, wqn_ref, wqa_ref, wqb_ref, wuk_ref, wuv_ref,
                   qn_ref, qr_ref, kn_ref, v_ref, *, heads):
    cqn = cqn_ref[...]
    c16 = c_ref[...].astype(BF16)
    cs = jnp.concatenate([cs_ref[...]] * (COL_TILE // LANES), axis=1)
    sn = jnp.concatenate([sn_ref[...]] * (COL_TILE // LANES), axis=1)
    for a, b in _col_tiles(heads * LANES):
        qn_ref[:, a:b] = _dot(cqn, wqn_ref[:, a:b]).astype(BF16)
        qr_ref[:, a:b] = (_dot(cqn, wqa_ref[:, a:b]) * cs[:, :b - a]
                          + _dot(cqn, wqb_ref[:, a:b]) * sn[:, :b - a]).astype(BF16)
        kn_ref[:, a:b] = _dot(c16, wuk_ref[:, a:b]).astype(BF16)
        v_ref[:, a:b] = _dot(c16, wuv_ref[:, a:b]).astype(BF16)


def _mla_prompt_attn_kernel(qn_ref, qr_ref, kn_ref, kr_ref, v_ref, gate_ref, o_ref, *, chunks, pad, scale):
    def keys(a, b):
        return jnp.concatenate([kn_ref[a:b, :], kr_ref[a:b, :]], axis=-1)

    for r0, r1 in chunks:
        nq = r1 - r0
        q = jnp.concatenate([qn_ref[r0:r1, :], qr_ref[r0:r1, :]], axis=-1)
        d0 = max(r0, pad)
        s_d = _dot_nt(q, keys(d0, r1)) * scale
        row = r0 + lax.broadcasted_iota(jnp.int32, (nq, r1 - d0), 0)
        col = d0 + lax.broadcasted_iota(jnp.int32, (nq, r1 - d0), 1)
        s_d = jnp.where(col <= row, s_d, NEG_INF)
        m = jnp.max(s_d, axis=-1, keepdims=True)
        if d0 > pad:
            s_f = _dot_nt(q, keys(pad, d0)) * scale
            m = jnp.maximum(m, jnp.max(s_f, axis=-1, keepdims=True))
        p_d = jnp.exp(s_d - m)
        l = jnp.sum(p_d, axis=-1, keepdims=True)
        o = _dot(p_d.astype(BF16), v_ref[d0:r1, :])
        if d0 > pad:
            p_f = jnp.exp(s_f - m)
            l = l + jnp.sum(p_f, axis=-1, keepdims=True)
            o = o + _dot(p_f.astype(BF16), v_ref[pad:d0, :])
        o_ref[r0:r1, :] = (o * (1.0 / l) * _silu(gate_ref[r0:r1, :])).astype(BF16)


def _page_copies(pt_ref, hbm_refs, bufs, sems, seq, first_page, slot, npp, lookup):
    out = []
    for kk in range(npp):
        pid = pt_ref[seq, first_page + kk] if lookup else 0
        for a, (hbm, buf) in enumerate(zip(hbm_refs, bufs)):
            out.append(pltpu.make_async_copy(hbm.at[0, pid], buf.at[slot, kk], sems.at[a, slot]))
    return out


def _start_pages(pt_ref, hbm_refs, bufs, sems, seq, first_page, slot, npp):
    for i, cp in enumerate(_page_copies(pt_ref, hbm_refs, bufs, sems, seq, first_page, slot, npp, True)):
        cp.start(priority=i % 2)


def _wait_pages(pt_ref, hbm_refs, bufs, sems, slot, npp):
    for cp in _page_copies(pt_ref, hbm_refs, bufs, sems, 0, 0, slot, npp, False):
        cp.wait()


def _ring_loop(pt_ref, hbm_refs, bufs, sems, seq, nseq, n, npp, first_page_of, compute, *, across_sequences):
    slots = bufs[0].shape[0]
    ahead = slots - 1

    def start(sq, chunk, slot):
        _start_pages(pt_ref, hbm_refs, bufs, sems, sq, first_page_of(chunk), slot, npp)

    assert n >= ahead or not across_sequences
    base = seq * n if across_sequences else 0
    for i in range(min(ahead, n)):
        if across_sequences:
            @pl.when(seq == 0)
            def _(i=i):
                start(seq, i, i)
        else:
            start(seq, i, i)

    def body(c, carry):
        t = base + c
        slot = lax.rem(t, slots)
        c2 = c + ahead
        slot2 = lax.rem(t + ahead, slots)

        @pl.when(c2 < n)
        def _():
            start(seq, c2, slot2)

        if across_sequences:
            @pl.when((c2 >= n) & (seq + 1 < nseq))
            def _():
                start(seq + 1, c2 - n, slot2)

        _wait_pages(pt_ref, hbm_refs, bufs, sems, slot, npp)
        compute(slot)
        return carry

    lax.fori_loop(0, n, body, 0)


def _mla_qlat_kernel(qn_ref, wukt_ref, o_ref):
    o_ref[...] = _dot(qn_ref[...], wukt_ref[...]).astype(BF16)


def _token_of_row(row, rows_per_token, n_tokens):
    t = jnp.zeros(row.shape, jnp.int32)
    for i in range(1, n_tokens):
        t = t + (row >= i * rows_per_token).astype(jnp.int32)
    return t


def _mla_sample_attn_kernel(pt_ref, ql_ref, qr_ref, cnew_ref, krnewt_ref, lat_hbm, krt_hbm, o_ref,
                            lat_buf, krt_buf, sems, m_ref, l_ref, acc_ref,
                            *, npp, nchunk, t_new, heads, scale):
    ql = ql_ref[...]
    qr = qr_ref[...]
    nrow = ql.shape[0]
    page = lat_buf.shape[2]
    lkv = lat_buf.shape[3]

    nstream = m_ref.shape[0]

    def update(parts):
        ss = []
        for _, c16, krt16, mask in parts:
            s = (_dot_nt(ql, c16) + _dot(qr, krt16)) * scale
            ss.append(s if mask is None else jnp.where(mask, s, NEG_INF))
        for (i, c16, _, _), s in zip(parts, ss):
            m_old = m_ref[i]
            m_new = jnp.maximum(m_old, jnp.max(s, axis=-1, keepdims=True))
            alpha = jnp.exp(m_old - m_new)
            p = jnp.exp(s - m_new)
            l_ref[i] = l_ref[i] * alpha + jnp.sum(p, axis=-1, keepdims=True)
            acc_ref[i] = acc_ref[i] * alpha + _dot(p.astype(BF16), c16)
            m_ref[i] = m_new

    m_ref[...] = jnp.full(m_ref.shape, NEG_INF, F32)
    l_ref[...] = jnp.zeros(l_ref.shape, F32)
    acc_ref[...] = jnp.zeros(acc_ref.shape, F32)
    nn = cnew_ref.shape[0]
    cnew16 = jnp.concatenate([cnew_ref[...], jnp.zeros((page - nn, lkv), F32)], axis=0).astype(BF16)
    tq = _token_of_row(lax.broadcasted_iota(jnp.int32, (nrow, page), 0), heads, t_new)
    col = lax.broadcasted_iota(jnp.int32, (nrow, page), 1)
    update([(0, cnew16, krnewt_ref[...].astype(BF16), col <= tq)])

    sub = npp // nstream

    def compute(slot):
        parts = []
        for i in range(nstream):
            c16 = lat_buf[slot, i * sub:(i + 1) * sub].reshape(sub * page, lkv).astype(BF16)
            krt16 = jnp.concatenate([krt_buf[slot, i * sub + k].astype(BF16) for k in range(sub)], axis=1)
            parts.append((i, c16, krt16, None))
        update(parts)

    _ring_loop(pt_ref, (lat_hbm, krt_hbm), (lat_buf, krt_buf), sems, pl.program_id(0), pl.num_programs(0),
               nchunk, npp, lambda c: c * npp, compute, across_sequences=True)
    m = m_ref[0]
    for i in range(1, nstream):
        m = jnp.maximum(m, m_ref[i])
    num = jnp.zeros(acc_ref.shape[1:], F32)
    den = jnp.zeros(l_ref.shape[1:], F32)
    for i in range(nstream):
        w = jnp.exp(m_ref[i] - m)
        num = num + acc_ref[i] * w
        den = den + l_ref[i] * w
    o_ref[...] = (num * (1.0 / den)).astype(BF16)


def _mla_sample_out_kernel(ol_ref, wuv_ref, gate_ref, o_ref):
    o = _dot(ol_ref[...], wuv_ref[...])
    o_ref[...] = (o * _silu(gate_ref[...])).astype(BF16)


def _outproj_kernel(h_ref, ogp_ref, ogs_ref, w_ref, o_ref, *, n_prompt_blocks):
    i = pl.program_id(0)

    @pl.when(i < n_prompt_blocks)
    def _():
        o_ref[...] = h_ref[...] + _dot(ogp_ref[...], w_ref[...])

    @pl.when(i >= n_prompt_blocks)
    def _():
        o_ref[...] = h_ref[...] + _dot(ogs_ref[...], w_ref[...])


def _sb_in_kernel(h_ref, g_ref, w_ref, q_ref, k_ref, v_ref, gate_ref, *, nbr, kvw):
    xn = _rms(h_ref[...], g_ref[...]).astype(BF16)
    for a, b in _col_tiles(nbr):
        q_ref[:, a:b] = _dot(xn, w_ref[:, a:b]).astype(BF16)
    k_ref[...] = _dot(xn, w_ref[:, nbr:nbr + kvw])
    v_ref[...] = _dot(xn, w_ref[:, nbr + kvw:nbr + 2 * kvw])
    o = nbr + 2 * kvw
    for a, b in _col_tiles(nbr):
        gate_ref[:, a:b] = _dot(xn, w_ref[:, o + a:o + b])


def _outproj_final_kernel(h_ref, og_ref, w_ref, g_ref, y_ref):
    h = h_ref[...] + _dot(og_ref[...], w_ref[...])
    y_ref[...] = _rms(h, g_ref[...])


def _drop_log2(u):
    return jnp.maximum(u, 0.0) + jnp.log2(1.0 + jnp.exp2(-jnp.abs(u)))


def _suffix_sums(x, tri):
    hi = x.astype(BF16)
    lo = (x - hi.astype(F32)).astype(BF16)
    return _dot(hi, tri) + _dot(lo, tri)


def _sb_span(u, v16, tri, masks, used, kb):
    nq, nk = u.shape
    nblk = nk // kb
    drop = _drop_log2(u)
    blocks = []
    for b in range(nblk):
        blk = drop[:, b * kb:(b + 1) * kb]
        blocks.append(blk if masks[b] is None else jnp.where(masks[b], blk, 0.0))
    ssum = _suffix_sums(jnp.concatenate(blocks, axis=0), tri)
    a_blocks = [None] * nblk
    for b in reversed(range(nblk)):
        sb = ssum[b * nq:(b + 1) * nq, :]
        a = jnp.exp2(u[:, b * kb:(b + 1) * kb] - sb - used)
        if masks[b] is not None:
            a = jnp.where(masks[b], a, 0.0)
        a_blocks[b] = a.astype(BF16)
        used = used + sb[:, 0:1]
    return _dot(jnp.concatenate(a_blocks, axis=1), v16), used


def _sb_prompt_attn_kernel(q_ref, k_ref, v_ref, gate_ref, tri_ref, o_ref, acc_ref, used_ref,
                           *, chunks, pad, scale2, kb):
    hd = q_ref.shape[1]
    tri = tri_ref[...]
    fb = LANES
    far = []
    for r0, r1 in chunks:
        nq = r1 - r0
        nblk = min(-(-(r1 - r0 + kb) // kb), -(-r1 // kb))
        w0 = r1 - nblk * kb
        ext = max(-w0, 0)
        w0 = max(w0, 0)
        k16 = k_ref[w0:r1, :].astype(BF16)
        v16 = v_ref[w0:r1, :].astype(BF16)
        if ext:
            k16 = jnp.concatenate([jnp.zeros((ext, hd), BF16), k16], axis=0)
            v16 = jnp.concatenate([jnp.zeros((ext, hd), BF16), v16], axis=0)
        masks = []
        for b in range(nblk):
            c0 = w0 - ext + b * kb
            if c0 + kb <= r0 and c0 >= pad:
                masks.append(None)
            else:
                row = r0 + lax.broadcasted_iota(jnp.int32, (nq, kb), 0)
                col = c0 + lax.broadcasted_iota(jnp.int32, (nq, kb), 1)
                masks.append((col < row) & (col >= pad))
        o, used = _sb_span(_dot_nt(q_ref[r0:r1, :], k16) * scale2, v16, tri, masks, jnp.zeros((nq, 1), F32), kb)
        acc_ref[r0:r1, :] = o
        used_ref[r0:r1, :] = jnp.broadcast_to(used, (nq, LANES))
        if w0 > 0:
            far.append((r0, r1, w0))

    for r0, r1, w0 in far:
        nq = r1 - r0

        @pl.when(jnp.min(used_ref[r0:r1, :]) < DEAD_MASS_LOG2)
        def _(r0=r0, r1=r1, w0=w0, nq=nq):
            q = q_ref[r0:r1, :]

            def far_block(i, carry):
                c0 = pl.multiple_of(w0 - (i + 1) * fb, fb)
                kf = k_ref[pl.ds(c0, fb), :].astype(BF16)
                vf = v_ref[pl.ds(c0, fb), :].astype(BF16)
                col = c0 + lax.broadcasted_iota(jnp.int32, (nq, fb), 1)
                of, uf = _sb_span(_dot_nt(q, kf) * scale2, vf, tri_ref[0:fb, 0:fb], [col >= pad],
                                  used_ref[r0:r1, 0:1], fb)
                acc_ref[r0:r1, :] += of
                used_ref[r0:r1, :] = jnp.broadcast_to(uf, (nq, LANES))
                return carry

            lax.fori_loop(0, w0 // fb, far_block, 0)

    o_ref[...] = (acc_ref[...] * _silu(gate_ref[...])).astype(BF16)


def _sb_sample_attn_kernel(pt_ref, q_ref, knew_ref, vnew_ref, tri_ref, k_hbm, v_hbm, o_ref,
                           k0_buf, v0_buf, sems0, k_buf, v_buf, sems, used_ref, acc_ref,
                           *, npp, nchunk, n_pages, groups, scale2, t_new, kb):
    seq = pl.program_id(0)
    nseq = pl.num_programs(0)
    rpg = q_ref.shape[1]
    hd = q_ref.shape[2]
    page = k_buf.shape[2] // groups
    nrow = groups * rpg
    qs = [q_ref[g] for g in range(groups)]
    first_page_of = lambda c: n_pages - (c + 1) * npp

    slot0 = lax.rem(seq, 2)

    @pl.when(seq == 0)
    def _():
        _start_pages(pt_ref, (k_hbm, v_hbm), (k0_buf, v0_buf), sems0, seq, first_page_of(0), 0, npp)

    @pl.when(seq + 1 < nseq)
    def _():
        _start_pages(pt_ref, (k_hbm, v_hbm), (k0_buf, v0_buf), sems0, seq + 1, first_page_of(0), 1 - slot0, npp)

    def span(kg, vg, tri, masks, width):
        u = jnp.concatenate([_dot_nt(qs[g], kg[g]) for g in range(groups)], axis=0) * scale2
        nblk = u.shape[1] // width
        drop = _drop_log2(u)
        blocks = []
        for b in range(nblk):
            blk = drop[:, b * width:(b + 1) * width]
            blocks.append(blk if masks is None else jnp.where(masks[b], blk, 0.0))
        ssum = _suffix_sums(jnp.concatenate(blocks, axis=0), tri)
        used = used_ref[:, 0:1]
        a_blocks = [None] * nblk
        for b in reversed(range(nblk)):
            sb = ssum[b * nrow:(b + 1) * nrow, :]
            a = jnp.exp2(u[:, b * width:(b + 1) * width] - sb - used)
            if masks is not None:
                a = jnp.where(masks[b], a, 0.0)
            a_blocks[b] = a.astype(BF16)
            used = used + sb[:, 0:1]
        a16 = jnp.concatenate(a_blocks, axis=1)
        acc_ref[...] += jnp.concatenate([_dot(a16[g * rpg:(g + 1) * rpg, :], vg[g]) for g in range(groups)], axis=0)
        used_ref[...] = jnp.broadcast_to(used, used_ref.shape)

    acc_ref[...] = jnp.zeros(acc_ref.shape, F32)
    used_ref[...] = jnp.zeros(used_ref.shape, F32)
    nn = knew_ref.shape[0]
    zpad = jnp.zeros((page - nn, hd), F32)
    kg = [jnp.concatenate([knew_ref[:, g * hd:(g + 1) * hd], zpad], axis=0).astype(BF16) for g in range(groups)]
    vg = [jnp.concatenate([vnew_ref[:, g * hd:(g + 1) * hd], zpad], axis=0).astype(BF16) for g in range(groups)]
    tq1 = _token_of_row(lax.broadcasted_iota(jnp.int32, (rpg, page), 0), rpg // t_new, t_new)
    tq = jnp.concatenate([tq1] * groups, axis=0)
    span(kg, vg, tri_ref[0:page, 0:page], [lax.broadcasted_iota(jnp.int32, (nrow, page), 1) < tq], page)

    def chunk(kb_ref, vb_ref, slot):
        kg = [jnp.concatenate([kb_ref[slot, kk, pl.ds(g, page, stride=groups), :] for kk in range(npp)],
                              axis=0).astype(BF16) for g in range(groups)]
        vg = [jnp.concatenate([vb_ref[slot, kk, pl.ds(g, page, stride=groups), :] for kk in range(npp)],
                              axis=0).astype(BF16) for g in range(groups)]
        span(kg, vg, tri_ref[...], None, kb)

    _wait_pages(pt_ref, (k_hbm, v_hbm), (k0_buf, v0_buf), sems0, slot0, npp)
    chunk(k0_buf, v0_buf, slot0)

    @pl.when(jnp.min(used_ref[...]) < DEAD_MASS_LOG2)
    def _():
        _ring_loop(pt_ref, (k_hbm, v_hbm), (k_buf, v_buf), sems, seq, nseq, nchunk - 1, npp,
                   lambda c: first_page_of(c + 1), functools.partial(chunk, k_buf, v_buf),
                   across_sequences=False)

    o_ref[...] = acc_ref[...]


def _gate_kernel(o_ref, gate_ref, og_ref):
    og_ref[...] = (o_ref[...] * _silu(gate_ref[...])).astype(BF16)


def _query_chunks(lp, size):
    n = max(lp // size, 1)
    bounds = [i * size for i in range(n)] + [lp]
    return tuple((bounds[i], bounds[i + 1]) for i in range(n))


def _sb_query_chunks(lp, size):
    return tuple((a, min(a + size, lp)) for a in range(0, lp, size))


def kernel(x_prompt, x_sample, cache_mla_latent, cache_mla_krope, cache_sb_k, cache_sb_v, page_table,
           meta_tokens, mla_norm, mla_w_in, mla_q_norm, mla_kv_norm, mla_w_uq, mla_w_uk, mla_w_uv, mla_w_out,
           sb_norm, sb_w_in, sb_w_out, final_norm):
    B, S, D = x_prompt.shape
    NB, T = x_sample.shape[:2]
    n_meta = meta_tokens.shape[0]
    n_pages = page_table.shape[1]
    page = cache_mla_latent.shape[2]
    lq = mla_q_norm.shape[-1]
    lkv = mla_kv_norm.shape[-1]
    H = mla_w_uq.shape[2]
    nope = mla_w_uk.shape[-1]
    rope = mla_w_uq.shape[-1] - nope
    vd = mla_w_uv.shape[-1]
    nbr = H * vd
    half = rope // 2
    G = cache_sb_k.shape[3]
    hd = cache_sb_k.shape[4]
    kvw = G * hd
    sb_nbr = sb_w_out.shape[1]
    SH = sb_nbr // hd
    HG = SH // G
    npp = PAGES_PER_CHUNK
    assert nope == LANES and vd == LANES and hd == LANES and page == LANES and 2 * rope == LANES
    assert mla_norm.shape[0] == 1 and sb_norm.shape[0] == 1, "one layer of each mixer"
    assert n_pages % npp == 0 and npp % SOFTMAX_STREAMS == 0 and (npp * page) % SB_KEY_BLOCK == 0 and T <= 8

    L = n_meta + S
    pad = (-L) % ROW_ALIGN
    LP = L + pad
    RP = B * LP
    RS = NB * T
    R = RP + RS
    tm = ROW_TILE
    assert RP % tm == 0 and RS % tm == 0 and RP % RS == 0 and S % ROW_ALIGN == 0 and pad % BF16_ROWS == 0
    nblk = R // tm
    past = n_pages * page

    front = [jnp.zeros((pad, D), F32), meta_tokens.astype(F32)]
    h0 = jnp.concatenate([p for b in range(B) for p in front + [x_prompt[b]]] + [x_sample.reshape(RS, D)], axis=0)

    inv = ROPE_THETA ** (-jnp.arange(half, dtype=F32) / half)

    def rope_tables(pos, reps):
        ang = pos.astype(F32)[:, None] * inv[None, :]
        zl = jnp.zeros((pos.shape[0], LANES - rope), F32)
        return (jnp.tile(jnp.concatenate([jnp.cos(ang), jnp.cos(ang), zl], axis=1), (reps, 1)),
                jnp.tile(jnp.concatenate([jnp.sin(ang), jnp.sin(ang), zl], axis=1), (reps, 1)))

    cs_p, sn_p = rope_tables(jnp.arange(LP, dtype=jnp.int32) - pad, B)
    cs_s, sn_s = rope_tables(past + jnp.arange(T, dtype=jnp.int32), NB)
    cs = jnp.concatenate([cs_p, cs_s], axis=0)
    sn = jnp.concatenate([sn_p, sn_s], axis=0)

    def rot_cols(w):
        return jnp.concatenate([-w[..., half:], w[..., :half]], axis=-1)

    def lane_pad(w):
        return jnp.concatenate([w, jnp.zeros(w.shape[:-1] + (LANES - w.shape[-1],), w.dtype)], axis=-1)

    w_in = mla_w_in[0]
    o_kr = lq + lkv
    w_kr = w_in[:, o_kr:o_kr + rope]
    w_in_ext = jnp.concatenate([w_in[:, :o_kr], w_in[:, o_kr + rope:], lane_pad(w_kr), lane_pad(rot_cols(w_kr))],
                               axis=1).astype(BF16)
    n_in = w_in_ext.shape[1]
    w_uq = mla_w_uq[0]
    w_qn = w_uq[:, :, :nope].reshape(lq, H * nope).astype(BF16)
    w_qa = lane_pad(w_uq[:, :, nope:]).reshape(lq, H * LANES).astype(BF16)
    w_qb = lane_pad(rot_cols(w_uq[:, :, nope:])).reshape(lq, H * LANES).astype(BF16)
    w_uk = mla_w_uk[0].reshape(lkv, H * nope).astype(BF16)
    w_uv = mla_w_uv[0].reshape(lkv, H * vd).astype(BF16)
    w_ukt = jnp.transpose(mla_w_uk[0], (1, 2, 0)).astype(BF16)
    w_out0 = mla_w_out[0].astype(BF16)
    w_sb_in = sb_w_in[0].astype(BF16)
    w_out1 = sb_w_out[0].astype(BF16)
    g0 = mla_norm[0].reshape(1, D)
    gq = mla_q_norm[0].reshape(1, lq)
    gkv = mla_kv_norm[0].reshape(1, lkv)
    g1 = sb_norm[0].reshape(1, D)
    gf = final_norm.reshape(1, D)

    row_spec = lambda w: pl.BlockSpec((tm, w), lambda i: (i, 0))

    cqn, c_all, kr_all, krp_all, gate0 = pl.pallas_call(
        functools.partial(_mla_in_kernel, lq=lq, lkv=lkv, nbr=nbr, rope=rope),
        grid=(nblk,),
        in_specs=[row_spec(D), _const_spec((1, D)), _const_spec((D, n_in)), _const_spec((1, lq)),
                  _const_spec((1, lkv)), row_spec(LANES), row_spec(LANES)],
        out_specs=[row_spec(lq), row_spec(lkv), row_spec(rope), row_spec(LANES), row_spec(nbr)],
        out_shape=[jax.ShapeDtypeStruct((R, lq), BF16), jax.ShapeDtypeStruct((R, lkv), F32),
                   jax.ShapeDtypeStruct((R, rope), F32), jax.ShapeDtypeStruct((R, LANES), BF16),
                   jax.ShapeDtypeStruct((R, nbr), F32)],
        compiler_params=_cparams("arbitrary"), name="mla_in",
    )(h0, g0, w_in_ext, gq, gkv, cs, sn)

    qn_all, qr_all, kn_all, v_all = pl.pallas_call(
        functools.partial(_mla_up_kernel, heads=H),
        grid=(nblk,),
        in_specs=[row_spec(lq), row_spec(lkv), row_spec(LANES), row_spec(LANES),
                  _const_spec((lq, H * nope)), _const_spec((lq, H * LANES)), _const_spec((lq, H * LANES)),
                  _const_spec((lkv, H * nope)), _const_spec((lkv, H * vd))],
        out_specs=[row_spec(H * nope), row_spec(H * LANES), row_spec(H * nope), row_spec(H * vd)],
        out_shape=[jax.ShapeDtypeStruct((R, H * nope), BF16), jax.ShapeDtypeStruct((R, H * LANES), BF16),
                   jax.ShapeDtypeStruct((R, H * nope), BF16), jax.ShapeDtypeStruct((R, H * vd), BF16)],
        compiler_params=_cparams("arbitrary"), name="mla_up",
    )(cqn, c_all, cs, sn, w_qn, w_qa, w_qb, w_uk, w_uv)

    chunks = _query_chunks(LP, QUERY_CHUNK)
    mla_scale = float(nope + rope) ** -0.5
    head_spec = pl.BlockSpec((LP, LANES), lambda b, h: (b, h))
    ogp0 = pl.pallas_call(
        functools.partial(_mla_prompt_attn_kernel, chunks=chunks, pad=pad, scale=mla_scale),
        grid=(B, H),
        in_specs=[head_spec, head_spec, head_spec, pl.BlockSpec((LP, LANES), lambda b, h: (b, 0)),
                  head_spec, head_spec],
        out_specs=head_spec,
        out_shape=jax.ShapeDtypeStruct((RP, nbr), BF16),
        compiler_params=_cparams("arbitrary", "arbitrary"), name="mla_prompt_attn",
    )(qn_all, qr_all, kn_all, krp_all, v_all, gate0)

    sblk = RP // RS
    qlat = pl.pallas_call(
        _mla_qlat_kernel,
        grid=(H,),
        in_specs=[pl.BlockSpec((RS, nope), lambda h: (sblk, h)),
                  pl.BlockSpec((None, nope, lkv), lambda h: (h, 0, 0))],
        out_specs=pl.BlockSpec((None, RS, lkv), lambda h: (h, 0, 0)),
        out_shape=jax.ShapeDtypeStruct((H, RS, lkv), BF16),
        compiler_params=_cparams("arbitrary"), name="mla_qlat",
    )(qn_all, w_ukt)
    qlat_s = jnp.transpose(qlat.reshape(H, NB, T, lkv), (1, 2, 0, 3)).reshape(NB, T * H, lkv)
    qrope_s = qr_all[RP:].reshape(NB, T, H, LANES)[..., :rope].reshape(NB, T * H, rope)
    tp = 8
    cnew = jnp.pad(c_all[RP:].reshape(NB, T, lkv), ((0, 0), (0, tp - T), (0, 0)))
    krnew_t = jnp.pad(jnp.swapaxes(kr_all[RP:].reshape(NB, T, rope), 1, 2), ((0, 0), (0, 0), (0, page - T)))
    krope_t = jnp.swapaxes(cache_mla_krope, 2, 3)

    nchunk = n_pages // npp
    nrow = T * H
    seq_spec = lambda r, w: pl.BlockSpec((None, r, w), lambda s, pt: (s, 0, 0))
    any_spec = pl.BlockSpec(memory_space=pl.ANY)
    olat = pl.pallas_call(
        functools.partial(_mla_sample_attn_kernel, npp=npp, nchunk=nchunk, t_new=T, heads=H, scale=mla_scale),
        grid_spec=pltpu.PrefetchScalarGridSpec(
            num_scalar_prefetch=1, grid=(NB,),
            in_specs=[seq_spec(nrow, lkv), seq_spec(nrow, rope), seq_spec(tp, lkv), seq_spec(rope, page),
                      any_spec, any_spec],
            out_specs=seq_spec(nrow, lkv),
            scratch_shapes=[pltpu.VMEM((RING_SLOTS, npp, page, lkv), F32),
                            pltpu.VMEM((RING_SLOTS, npp, rope, page), F32),
                            pltpu.SemaphoreType.DMA((2, RING_SLOTS)),
                            pltpu.VMEM((SOFTMAX_STREAMS, nrow, 1), F32), pltpu.VMEM((SOFTMAX_STREAMS, nrow, 1), F32),
                            pltpu.VMEM((SOFTMAX_STREAMS, nrow, lkv), F32)]),
        out_shape=jax.ShapeDtypeStruct((NB, nrow, lkv), BF16),
        compiler_params=_cparams("arbitrary"), name="mla_sample_attn",
    )(page_table, qlat_s, qrope_s, cnew, krnew_t, cache_mla_latent, krope_t)
    olat_h = jnp.transpose(olat.reshape(NB, T, H, lkv), (2, 0, 1, 3)).reshape(H, RS, lkv)
    ogs0 = pl.pallas_call(
        _mla_sample_out_kernel,
        grid=(H,),
        in_specs=[pl.BlockSpec((None, RS, lkv), lambda h: (h, 0, 0)),
                  pl.BlockSpec((lkv, vd), lambda h: (0, h)),
                  pl.BlockSpec((RS, vd), lambda h: (sblk, h))],
        out_specs=pl.BlockSpec((RS, vd), lambda h: (0, h)),
        out_shape=jax.ShapeDtypeStruct((RS, nbr), BF16),
        compiler_params=_cparams("arbitrary"), name="mla_sample_out",
    )(olat_h, w_uv, gate0)

    npb = RP // tm
    h1 = pl.pallas_call(
        functools.partial(_outproj_kernel, n_prompt_blocks=npb),
        grid=(nblk,),
        in_specs=[row_spec(D),
                  pl.BlockSpec((tm, nbr), lambda i: (jnp.minimum(i, npb - 1), 0)),
                  pl.BlockSpec((tm, nbr), lambda i: (jnp.maximum(i - npb, 0), 0)),
                  _const_spec((nbr, D))],
        out_specs=row_spec(D),
        out_shape=jax.ShapeDtypeStruct((R, D), F32),
        compiler_params=_cparams("arbitrary"), name="outproj0",
    )(h0, ogp0, ogs0, w_out0)

    sbq, sbk, sbv, gate1 = pl.pallas_call(
        functools.partial(_sb_in_kernel, nbr=sb_nbr, kvw=kvw),
        grid=(nblk,),
        in_specs=[row_spec(D), _const_spec((1, D)), _const_spec((D, 2 * sb_nbr + 2 * kvw))],
        out_specs=[row_spec(sb_nbr), row_spec(kvw), row_spec(kvw), row_spec(sb_nbr)],
        out_shape=[jax.ShapeDtypeStruct((R, sb_nbr), BF16), jax.ShapeDtypeStruct((R, kvw), F32),
                   jax.ShapeDtypeStruct((R, kvw), F32), jax.ShapeDtypeStruct((R, sb_nbr), F32)],
        compiler_params=_cparams("arbitrary"), name="sb_in",
    )(h1, g1, w_sb_in)

    kb = SB_KEY_BLOCK
    tri = (lax.broadcasted_iota(jnp.int32, (kb, kb), 0) >= lax.broadcasted_iota(jnp.int32, (kb, kb), 1)).astype(BF16)
    sb_scale2 = float(hd) ** -0.5 * LOG2_E
    kv_spec = pl.BlockSpec((LP, hd), lambda b, h: (b, h // HG))
    ogp1 = pl.pallas_call(
        functools.partial(_sb_prompt_attn_kernel, chunks=_sb_query_chunks(LP, SB_QUERY_CHUNK), pad=pad,
                          scale2=sb_scale2, kb=kb),
        grid=(B, SH),
        in_specs=[head_spec, kv_spec, kv_spec, head_spec, pl.BlockSpec((kb, kb), lambda b, h: (0, 0))],
        out_specs=head_spec,
        scratch_shapes=[pltpu.VMEM((LP, hd), F32), pltpu.VMEM((LP, LANES), F32)],
        out_shape=jax.ShapeDtypeStruct((RP, sb_nbr), BF16),
        compiler_params=_cparams("arbitrary", "arbitrary"), name="sb_prompt_attn",
    )(sbq, sbk, sbv, gate1, tri)

    rpg = T * HG
    q_s = jnp.transpose(sbq[RP:].reshape(NB, T, G, HG, hd), (0, 2, 1, 3, 4)).reshape(NB, G, rpg, hd)
    knew = jnp.pad(sbk[RP:].reshape(NB, T, kvw), ((0, 0), (0, tp - T), (0, 0)))
    vnew = jnp.pad(sbv[RP:].reshape(NB, T, kvw), ((0, 0), (0, tp - T), (0, 0)))
    ck = cache_sb_k.reshape(cache_sb_k.shape[:2] + (page * G, hd))
    cv = cache_sb_v.reshape(cache_sb_v.shape[:2] + (page * G, hd))
    o_s = pl.pallas_call(
        functools.partial(_sb_sample_attn_kernel, npp=npp, nchunk=nchunk, n_pages=n_pages, groups=G,
                          scale2=sb_scale2, t_new=T, kb=kb),
        grid_spec=pltpu.PrefetchScalarGridSpec(
            num_scalar_prefetch=1, grid=(NB,),
            in_specs=[pl.BlockSpec((None, G, rpg, hd), lambda s, pt: (s, 0, 0, 0)),
                      seq_spec(tp, kvw), seq_spec(tp, kvw),
                      pl.BlockSpec((kb, kb), lambda s, pt: (0, 0)), any_spec, any_spec],
            out_specs=seq_spec(G * rpg, hd),
            scratch_shapes=[pltpu.VMEM((2, npp, page * G, hd), F32), pltpu.VMEM((2, npp, page * G, hd), F32),
                            pltpu.SemaphoreType.DMA((2, 2)),
                            pltpu.VMEM((RING_SLOTS, npp, page * G, hd), F32),
                            pltpu.VMEM((RING_SLOTS, npp, page * G, hd), F32),
                            pltpu.SemaphoreType.DMA((2, RING_SLOTS)),
                            pltpu.VMEM((G * rpg, LANES), F32), pltpu.VMEM((G * rpg, hd), F32)]),
        out_shape=jax.ShapeDtypeStruct((NB, G * rpg, hd), F32),
        compiler_params=_cparams("arbitrary"), name="sb_sample_attn",
    )(page_table, q_s, knew, vnew, tri, ck, cv)
    o_tok = jnp.transpose(o_s.reshape(NB, G, T, HG, hd), (0, 2, 1, 3, 4)).reshape(RS, sb_nbr)
    ogs1 = pl.pallas_call(
        _gate_kernel,
        grid=(1,),
        in_specs=[pl.BlockSpec((RS, sb_nbr), lambda i: (0, 0)), pl.BlockSpec((RS, sb_nbr), lambda i: (sblk, 0))],
        out_specs=pl.BlockSpec((RS, sb_nbr), lambda i: (0, 0)),
        out_shape=jax.ShapeDtypeStruct((RS, sb_nbr), BF16),
        compiler_params=_cparams("arbitrary"), name="sb_sample_gate",
    )(o_tok, gate1)

    bpl = LP // ROW_ALIGN
    spb = S // ROW_ALIGN
    tf = ROW_ALIGN
    y_prompt = pl.pallas_call(
        _outproj_final_kernel,
        grid=(B, spb),
        in_specs=[pl.BlockSpec((tf, D), lambda b, i: (b * bpl + (bpl - spb) + i, 0)),
                  pl.BlockSpec((tf, sb_nbr), lambda b, i: (b * bpl + (bpl - spb) + i, 0)),
                  _const_spec((sb_nbr, D)), _const_spec((1, D))],
        out_specs=pl.BlockSpec((None, tf, D), lambda b, i: (b, i, 0)),
        out_shape=jax.ShapeDtypeStruct((B, S, D), F32),
        compiler_params=_cparams("arbitrary", "arbitrary"), name="outproj1_prompt",
    )(h1, ogp1, w_out1, gf)
    y_sample = pl.pallas_call(
        _outproj_final_kernel,
        grid=(RS // tf,),
        in_specs=[pl.BlockSpec((tf, D), lambda i: (RP // tf + i, 0)),
                  pl.BlockSpec((tf, sb_nbr), lambda i: (i, 0)),
                  _const_spec((sb_nbr, D)), _const_spec((1, D))],
        out_specs=pl.BlockSpec((tf, D), lambda i: (i, 0)),
        out_shape=jax.ShapeDtypeStruct((RS, D), F32),
        compiler_params=_cparams("arbitrary"), name="outproj1_sample",
    )(h1, ogs1, w_out1, gf).reshape(NB, T, D)

    def prompt_rows(a):
        return a[:RP].reshape((B, LP) + a.shape[1:])[:, pad:][None]

    def sample_rows(a):
        return a[RP:].reshape((NB, T) + a.shape[1:])[None]

    sbk4 = sbk.reshape(R, G, hd)
    sbv4 = sbv.reshape(R, G, hd)
    return (y_prompt, y_sample,
            prompt_rows(c_all), prompt_rows(kr_all), prompt_rows(sbk4), prompt_rows(sbv4),
            sample_rows(c_all), sample_rows(kr_all), sample_rows(sbk4), sample_rows(sbv4))
```

```python
import functools

import jax
import jax.numpy as jnp
from jax import lax
from jax.experimental import pallas as pl
from jax.experimental.pallas import tpu as pltpu

EPS = 1e-6
NEG_INF = -1e30
ROPE_THETA = 10000.0
LANES = 128
BF16_ROWS = 16
ROW_ALIGN = 128
VMEM_LIMIT = 56 * 1024 * 1024
ROW_TILE = 512
COL_TILE = 1024
QUERY_CHUNK = 512
SB_QUERY_CHUNK = 256
PAGES_PER_CHUNK = 16
SB_PAGES_PER_CHUNK = 8
SOFTMAX_STREAMS = 4
RING_SLOTS = 3
SB_KEY_BLOCK = 256
LOG2_E = 1.4426950408889634
DEAD_MASS_LOG2 = 151.0

F32 = jnp.float32
BF16 = jnp.bfloat16


def _cparams(*sem):
    return pltpu.CompilerParams(dimension_semantics=sem, vmem_limit_bytes=VMEM_LIMIT)


def _rms(x, g):
    return x * lax.rsqrt(jnp.mean(x * x, axis=-1, keepdims=True) + EPS) * g


def _dot(a, b):
    return jnp.dot(a, b, preferred_element_type=F32)


def _dot_nt(a, b):
    return lax.dot_general(a, b, (((1,), (1,)), ((), ())), preferred_element_type=F32)


def _silu(x):
    return x * (1.0 / (1.0 + jnp.exp(-x)))


def _const_spec(shape):
    nd = len(shape)
    return pl.BlockSpec(shape, lambda *_: (0,) * nd, pipeline_mode=pl.Buffered(1))


def _col_tiles(n):
    return [(a, min(a + COL_TILE, n)) for a in range(0, n, COL_TILE)]


def _mla_in_kernel(h_ref, g_ref, w_ref, gq_ref, gkv_ref, cs_ref, sn_ref,
                   cqn_ref, c_ref, kr_ref, krp_ref, gate_ref, *, lq, lkv, nbr, rope):
    xn = _rms(h_ref[...], g_ref[...]).astype(BF16)
    cqn_ref[...] = _rms(_dot(xn, w_ref[:, :lq]), gq_ref[...]).astype(BF16)
    c_ref[...] = _rms(_dot(xn, w_ref[:, lq:lq + lkv]), gkv_ref[...])
    o = lq + lkv
    for a, b in _col_tiles(nbr):
        gate_ref[:, a:b] = _dot(xn, w_ref[:, o + a:o + b])
    o += nbr
    u = _dot(xn, w_ref[:, o:o + 2 * LANES])
    kr = u[:, :LANES] * cs_ref[...] + u[:, LANES:] * sn_ref[...]
    kr_ref[...] = kr[:, :rope]
    krp_ref[...] = kr.astype(BF16)


def _mla_up_kernel(cqn_ref, c_ref, cs_ref, sn_ref, wqn_ref, wqa_ref, wqb_ref, wuk_ref, wuv_ref,
                   qn_ref, qr_ref, kn_ref, v_ref, *, heads):
    cqn = cqn_ref[...]
    c16 = c_ref[...].astype(BF16)
    cs = jnp.concatenate([cs_ref[...]] * (COL_TILE // LANES), axis=1)
    sn = jnp.concatenate([sn_ref[...]] * (COL_TILE // LANES), axis=1)
    for a, b in _col_tiles(heads * LANES):
        qn_ref[:, a:b] = _dot(cqn, wqn_ref[:, a:b]).astype(BF16)
        qr_ref[:, a:b] = (_dot(cqn, wqa_ref[:, a:b]) * cs[:, :b - a]
                          + _dot(cqn, wqb_ref[:, a:b]) * sn[:, :b - a]).astype(BF16)
        kn_ref[:, a:b] = _dot(c16, wuk_ref[:, a:b]).astype(BF16)
        v_ref[:, a:b] = _dot(c16, wuv_ref[:, a:b]).astype(BF16)


def _mla_prompt_attn_kernel(qn_ref, qr_ref, kn_ref, kr_ref, v_ref, gate_ref, o_ref, *, chunks, pad, scale):
    def keys(a, b):
        return jnp.concatenate([kn_ref[a:b, :], kr_ref[a:b, :]], axis=-1)

    for r0, r1 in chunks:
        nq = r1 - r0
        q = jnp.concatenate([qn_ref[r0:r1, :], qr_ref[r0:r1, :]], axis=-1)
        d0 = max(r0, pad)
        s_d = _dot_nt(q, keys(d0, r1)) * scale
        row = r0 + lax.broadcasted_iota(jnp.int32, (nq, r1 - d0), 0)
        col = d0 + lax.broadcasted_iota(jnp.int32, (nq, r1 - d0), 1)
        s_d = jnp.where(col <= row, s_d, NEG_INF)
        m = jnp.max(s_d, axis=-1, keepdims=True)
        if d0 > pad:
            s_f = _dot_nt(q, keys(pad, d0)) * scale
            m = jnp.maximum(m, jnp.max(s_f, axis=-1, keepdims=True))
        p_d = jnp.exp(s_d - m)
        l = jnp.sum(p_d, axis=-1, keepdims=True)
        o = _dot(p_d.astype(BF16), v_ref[d0:r1, :])
        if d0 > pad:
            p_f = jnp.exp(s_f - m)
            l = l + jnp.sum(p_f, axis=-1, keepdims=True)
            o = o + _dot(p_f.astype(BF16), v_ref[pad:d0, :])
        o_ref[r0:r1, :] = (o * (1.0 / l) * _silu(gate_ref[r0:r1, :])).astype(BF16)


def _page_copies(pt_ref, hbm_refs, bufs, sems, seq, first_page, slot, npp, lookup):
    out = []
    for kk in range(npp):
        pid = pt_ref[seq, first_page + kk] if lookup else 0
        for a, (hbm, buf) in enumerate(zip(hbm_refs, bufs)):
            out.append(pltpu.make_async_copy(hbm.at[0, pid], buf.at[slot, kk], sems.at[a, slot]))
    return out


def _start_pages(pt_ref, hbm_refs, bufs, sems, seq, first_page, slot, npp):
    for i, cp in enumerate(_page_copies(pt_ref, hbm_refs, bufs, sems, seq, first_page, slot, npp, True)):
        cp.start(priority=i % 2)


def _wait_pages(pt_ref, hbm_refs, bufs, sems, slot, npp):
    for cp in _page_copies(pt_ref, hbm_refs, bufs, sems, 0, 0, slot, npp, False):
        cp.wait()


def _ring_loop(pt_ref, hbm_refs, bufs, sems, seq, nseq, n, npp, first_page_of, compute, *, across_sequences):
    slots = bufs[0].shape[0]
    ahead = slots - 1

    def start(sq, chunk, slot):
        _start_pages(pt_ref, hbm_refs, bufs, sems, sq, first_page_of(chunk), slot, npp)

    assert n >= ahead or not across_sequences
    base = seq * n if across_sequences else 0
    for i in range(min(ahead, n)):
        if across_sequences:
            @pl.when(seq == 0)
            def _(i=i):
                start(seq, i, i)
        else:
            start(seq, i, i)

    def body(c, carry):
        t = base + c
        slot = lax.rem(t, slots)
        c2 = c + ahead
        slot2 = lax.rem(t + ahead, slots)

        @pl.when(c2 < n)
        def _():
            start(seq, c2, slot2)

        if across_sequences:
            @pl.when((c2 >= n) & (seq + 1 < nseq))
            def _():
                start(seq + 1, c2 - n, slot2)

        _wait_pages(pt_ref, hbm_refs, bufs, sems, slot, npp)
        compute(slot)
        return carry

    lax.fori_loop(0, n, body, 0)


def _mla_qlat_kernel(qn_ref, wukt_ref, o_ref):
    o_ref[...] = _dot(qn_ref[...], wukt_ref[...]).astype(BF16)


def _token_of_row(row, rows_per_token, n_tokens):
    t = jnp.zeros(row.shape, jnp.int32)
    for i in range(1, n_tokens):
        t = t + (row >= i * rows_per_token).astype(jnp.int32)
    return t


def _mla_sample_attn_kernel(pt_ref, ql_ref, qr_ref, cnew_ref, krnewt_ref, lat_hbm, krt_hbm, o_ref,
                            lat_buf, krt_buf, sems, m_ref, l_ref, acc_ref,
                            *, npp, nchunk, t_new, heads, scale):
    ql = ql_ref[...]
    qr = qr_ref[...]
    nrow = ql.shape[0]
    page = lat_buf.shape[2]
    lkv = lat_buf.shape[3]

    nstream = m_ref.shape[0]

    def update(parts):
        ss = []
        for _, c16, krt16, mask in parts:
            s = (_dot_nt(ql, c16) + _dot(qr, krt16)) * scale
            ss.append(s if mask is None else jnp.where(mask, s, NEG_INF))
        for (i, c16, _, _), s in zip(parts, ss):
            m_old = m_ref[i]
            m_new = jnp.maximum(m_old, jnp.max(s, axis=-1, keepdims=True))
            alpha = jnp.exp(m_old - m_new)
            p = jnp.exp(s - m_new)
            l_ref[i] = l_ref[i] * alpha + jnp.sum(p, axis=-1, keepdims=True)
            acc_ref[i] = acc_ref[i] * alpha + _dot(p.astype(BF16), c16)
            m_ref[i] = m_new

    m_ref[...] = jnp.full(m_ref.shape, NEG_INF, F32)
    l_ref[...] = jnp.zeros(l_ref.shape, F32)
    acc_ref[...] = jnp.zeros(acc_ref.shape, F32)
    nn = cnew_ref.shape[0]
    cnew16 = jnp.concatenate([cnew_ref[...], jnp.zeros((page - nn, lkv), F32)], axis=0).astype(BF16)
    tq = _token_of_row(lax.broadcasted_iota(jnp.int32, (nrow, page), 0), heads, t_new)
    col = lax.broadcasted_iota(jnp.int32, (nrow, page), 1)
    update([(0, cnew16, krnewt_ref[...].astype(BF16), col <= tq)])

    sub = npp // nstream

    def compute(slot):
        parts = []
        for i in range(nstream):
            c16 = lat_buf[slot, i * sub:(i + 1) * sub].reshape(sub * page, lkv).astype(BF16)
            krt16 = jnp.concatenate([krt_buf[slot, i * sub + k].astype(BF16) for k in range(sub)], axis=1)
            parts.append((i, c16, krt16, None))
        update(parts)

    _ring_loop(pt_ref, (lat_hbm, krt_hbm), (lat_buf, krt_buf), sems, pl.program_id(0), pl.num_programs(0),
               nchunk, npp, lambda c: c * npp, compute, across_sequences=True)
    m = m_ref[0]
    for i in range(1, nstream):
        m = jnp.maximum(m, m_ref[i])
    num = jnp.zeros(acc_ref.shape[1:], F32)
    den = jnp.zeros(l_ref.shape[1:], F32)
    for i in range(nstream):
        w = jnp.exp(m_ref[i] - m)
        num = num + acc_ref[i] * w
        den = den + l_ref[i] * w
    o_ref[...] = (num * (1.0 / den)).astype(BF16)


def _mla_sample_out_kernel(ol_ref, wuv_ref, gate_ref, o_ref):
    o = _dot(ol_ref[...], wuv_ref[...])
    o_ref[...] = (o * _silu(gate_ref[...])).astype(BF16)


def _outproj_kernel(h_ref, ogp_ref, ogs_ref, w_ref, o_ref, *, n_prompt_blocks):
    i = pl.program_id(0)

    @pl.when(i < n_prompt_blocks)
    def _():
        o_ref[...] = h_ref[...] + _dot(ogp_ref[...], w_ref[...])

    @pl.when(i >= n_prompt_blocks)
    def _():
        o_ref[...] = h_ref[...] + _dot(ogs_ref[...], w_ref[...])


def _sb_in_kernel(h_ref, g_ref, w_ref, q_ref, k_ref, v_ref, gate_ref, *, nbr, kvw):
    xn = _rms(h_ref[...], g_ref[...]).astype(BF16)
    for a, b in _col_tiles(nbr):
        q_ref[:, a:b] = _dot(xn, w_ref[:, a:b]).astype(BF16)
    k_ref[...] = _dot(xn, w_ref[:, nbr:nbr + kvw])
    v_ref[...] = _dot(xn, w_ref[:, nbr + kvw:nbr + 2 * kvw])
    o = nbr + 2 * kvw
    for a, b in _col_tiles(nbr):
        gate_ref[:, a:b] = _dot(xn, w_ref[:, o + a:o + b])


def _outproj_final_kernel(h_ref, og_ref, w_ref, g_ref, y_ref):
    h = h_ref[...] + _dot(og_ref[...], w_ref[...])
    y_ref[...] = _rms(h, g_ref[...])


def _drop_log2(u):
    return jnp.maximum(u, 0.0) + jnp.log2(1.0 + jnp.exp2(-jnp.abs(u)))


def _suffix_sums(x, tri):
    hi = x.astype(BF16)
    lo = (x - hi.astype(F32)).astype(BF16)
    return _dot(hi, tri) + _dot(lo, tri)


def _sb_span(u, v16, tri, masks, used, kb):
    nq, nk = u.shape
    nblk = nk // kb
    drop = _drop_log2(u)
    blocks = []
    for b in range(nblk):
        blk = drop[:, b * kb:(b + 1) * kb]
        blocks.append(blk if masks[b] is None else jnp.where(masks[b], blk, 0.0))
    ssum = _suffix_sums(jnp.concatenate(blocks, axis=0), tri)
    a_blocks = [None] * nblk
    for b in reversed(range(nblk)):
        sb = ssum[b * nq:(b + 1) * nq, :]
        a = jnp.exp2(u[:, b * kb:(b + 1) * kb] - sb - used)
        if masks[b] is not None:
            a = jnp.where(masks[b], a, 0.0)
        a_blocks[b] = a.astype(BF16)
        used = used + sb[:, 0:1]
    return _dot(jnp.concatenate(a_blocks, axis=1), v16), used


def _sb_prompt_attn_kernel(q_ref, k_ref, v_ref, gate_ref, tri_ref, o_ref, acc_ref, used_ref,
                           *, chunks, pad, scale2, kb):
    hd = q_ref.shape[1]
    tri = tri_ref[...]
    fb = LANES
    far = []
    for r0, r1 in chunks:
        nq = r1 - r0
        nblk = min(-(-(r1 - r0 + kb) // kb), -(-r1 // kb))
        w0 = r1 - nblk * kb
        ext = max(-w0, 0)
        w0 = max(w0, 0)
        k16 = k_ref[w0:r1, :].astype(BF16)
        v16 = v_ref[w0:r1, :].astype(BF16)
        if ext:
            k16 = jnp.concatenate([jnp.zeros((ext, hd), BF16), k16], axis=0)
            v16 = jnp.concatenate([jnp.zeros((ext, hd), BF16), v16], axis=0)
        masks = []
        for b in range(nblk):
            c0 = w0 - ext + b * kb
            if c0 + kb <= r0 and c0 >= pad:
                masks.append(None)
            else:
                row = r0 + lax.broadcasted_iota(jnp.int32, (nq, kb), 0)
                col = c0 + lax.broadcasted_iota(jnp.int32, (nq, kb), 1)
                masks.append((col < row) & (col >= pad))
        o, used = _sb_span(_dot_nt(q_ref[r0:r1, :], k16) * scale2, v16, tri, masks, jnp.zeros((nq, 1), F32), kb)
        acc_ref[r0:r1, :] = o
        used_ref[r0:r1, :] = jnp.broadcast_to(used, (nq, LANES))
        if w0 > 0:
            far.append((r0, r1, w0))

    for r0, r1, w0 in far:
        nq = r1 - r0

        @pl.when(jnp.min(used_ref[r0:r1, :]) < DEAD_MASS_LOG2)
        def _(r0=r0, r1=r1, w0=w0, nq=nq):
            q = q_ref[r0:r1, :]

            def far_block(i, carry):
                c0 = pl.multiple_of(w0 - (i + 1) * fb, fb)
                kf = k_ref[pl.ds(c0, fb), :].astype(BF16)
                vf = v_ref[pl.ds(c0, fb), :].astype(BF16)
                col = c0 + lax.broadcasted_iota(jnp.int32, (nq, fb), 1)
                of, uf = _sb_span(_dot_nt(q, kf) * scale2, vf, tri_ref[0:fb, 0:fb], [col >= pad],
                                  used_ref[r0:r1, 0:1], fb)
                acc_ref[r0:r1, :] += of
                used_ref[r0:r1, :] = jnp.broadcast_to(uf, (nq, LANES))
                return carry

            lax.fori_loop(0, w0 // fb, far_block, 0)

    o_ref[...] = (acc_ref[...] * _silu(gate_ref[...])).astype(BF16)


def _sb_sample_attn_kernel(pt_ref, q_ref, knew_ref, vnew_ref, tri_ref, k_hbm, v_hbm, o_ref,
                           k0_buf, v0_buf, sems0, k_buf, v_buf, sems, used_ref, acc_ref,
                           *, npp, nchunk, n_pages, groups, scale2, t_new, kb):
    seq = pl.program_id(0)
    nseq = pl.num_programs(0)
    rpg = q_ref.shape[1]
    hd = q_ref.shape[2]
    page = k_buf.shape[2] // groups
    nrow = groups * rpg
    qs = [q_ref[g] for g in range(groups)]
    first_page_of = lambda c: n_pages - (c + 1) * npp

    slot0 = lax.rem(seq, 2)

    @pl.when(seq == 0)
    def _():
        _start_pages(pt_ref, (k_hbm, v_hbm), (k0_buf, v0_buf), sems0, seq, first_page_of(0), 0, npp)

    @pl.when(seq + 1 < nseq)
    def _():
        _start_pages(pt_ref, (k_hbm, v_hbm), (k0_buf, v0_buf), sems0, seq + 1, first_page_of(0), 1 - slot0, npp)

    def span(kg, vg, tri, masks, width):
        u = jnp.concatenate([_dot_nt(qs[g], kg[g]) for g in range(groups)], axis=0) * scale2
        nblk = u.shape[1] // width
        drop = _drop_log2(u)
        blocks = []
        for b in range(nblk):
            blk = drop[:, b * width:(b + 1) * width]
            blocks.append(blk if masks is None else jnp.where(masks[b], blk, 0.0))
        ssum = _suffix_sums(jnp.concatenate(blocks, axis=0), tri)
        used = used_ref[:, 0:1]
        a_blocks = [None] * nblk
        for b in reversed(range(nblk)):
            sb = ssum[b * nrow:(b + 1) * nrow, :]
            a = jnp.exp2(u[:, b * width:(b + 1) * width] - sb - used)
            if masks is not None:
                a = jnp.where(masks[b], a, 0.0)
            a_blocks[b] = a.astype(BF16)
            used = used + sb[:, 0:1]
        a16 = jnp.concatenate(a_blocks, axis=1)
        acc_ref[...] += jnp.concatenate([_dot(a16[g * rpg:(g + 1) * rpg, :], vg[g]) for g in range(groups)], axis=0)
        used_ref[...] = jnp.broadcast_to(used, used_ref.shape)

    acc_ref[...] = jnp.zeros(acc_ref.shape, F32)
    used_ref[...] = jnp.zeros(used_ref.shape, F32)
    nn = knew_ref.shape[0]
    zpad = jnp.zeros((page - nn, hd), F32)
    kg = [jnp.concatenate([knew_ref[:, g * hd:(g + 1) * hd], zpad], axis=0).astype(BF16) for g in range(groups)]
    vg = [jnp.concatenate([vnew_ref[:, g * hd:(g + 1) * hd], zpad], axis=0).astype(BF16) for g in range(groups)]
    tq1 = _token_of_row(lax.broadcasted_iota(jnp.int32, (rpg, page), 0), rpg // t_new, t_new)
    tq = jnp.concatenate([tq1] * groups, axis=0)
    span(kg, vg, tri_ref[0:page, 0:page], [lax.broadcasted_iota(jnp.int32, (nrow, page), 1) < tq], page)

    def chunk(kb_ref, vb_ref, slot):
        kg = [jnp.concatenate([kb_ref[slot, kk, pl.ds(g, page, stride=groups), :] for kk in range(npp)],
                              axis=0).astype(BF16) for g in range(groups)]
        vg = [jnp.concatenate([vb_ref[slot, kk, pl.ds(g, page, stride=groups), :] for kk in range(npp)],
                              axis=0).astype(BF16) for g in range(groups)]
        span(kg, vg, tri_ref[...], None, kb)

    _wait_pages(pt_ref, (k_hbm, v_hbm), (k0_buf, v0_buf), sems0, slot0, npp)
    chunk(k0_buf, v0_buf, slot0)

    @pl.when(jnp.min(used_ref[...]) < DEAD_MASS_LOG2)
    def _():
        _ring_loop(pt_ref, (k_hbm, v_hbm), (k_buf, v_buf), sems, seq, nseq, nchunk - 1, npp,
                   lambda c: first_page_of(c + 1), functools.partial(chunk, k_buf, v_buf),
                   across_sequences=False)

    o_ref[...] = acc_ref[...]


def _gate_kernel(o_ref, gate_ref, og_ref):
    og_ref[...] = (o_ref[...] * _silu(gate_ref[...])).astype(BF16)


def _query_chunks(lp, size):
    n = max(lp // size, 1)
    bounds = [i * size for i in range(n)] + [lp]
    return tuple((bounds[i], bounds[i + 1]) for i in range(n))


def _sb_query_chunks(lp, size):
    return tuple((a, min(a + size, lp)) for a in range(0, lp, size))


def kernel(x_prompt, x_sample, cache_mla_latent, cache_mla_krope, cache_sb_k, cache_sb_v, page_table,
           meta_tokens, mla_norm, mla_w_in, mla_q_norm, mla_kv_norm, mla_w_uq, mla_w_uk, mla_w_uv, mla_w_out,
           sb_norm, sb_w_in, sb_w_out, final_norm):
    B, S, D = x_prompt.shape
    NB, T = x_sample.shape[:2]
    n_meta = meta_tokens.shape[0]
    n_pages = page_table.shape[1]
    page = cache_mla_latent.shape[2]
    lq = mla_q_norm.shape[-1]
    lkv = mla_kv_norm.shape[-1]
    H = mla_w_uq.shape[2]
    nope = mla_w_uk.shape[-1]
    rope = mla_w_uq.shape[-1] - nope
    vd = mla_w_uv.shape[-1]
    nbr = H * vd
    half = rope // 2
    G = cache_sb_k.shape[3]
    hd = cache_sb_k.shape[4]
    kvw = G * hd
    sb_nbr = sb_w_out.shape[1]
    SH = sb_nbr // hd
    HG = SH // G
    npp = PAGES_PER_CHUNK
    assert nope == LANES and vd == LANES and hd == LANES and page == LANES and 2 * rope == LANES
    assert mla_norm.shape[0] == 1 and sb_norm.shape[0] == 1, "one layer of each mixer"
    assert n_pages % npp == 0 and npp % SOFTMAX_STREAMS == 0 and (npp * page) % SB_KEY_BLOCK == 0 and T <= 8

    L = n_meta + S
    pad = (-L) % ROW_ALIGN
    LP = L + pad
    RP = B * LP
    RS = NB * T
    R = RP + RS
    tm = ROW_TILE
    assert RP % tm == 0 and RS % tm == 0 and RP % RS == 0 and S % ROW_ALIGN == 0 and pad % BF16_ROWS == 0
    nblk = R // tm
    past = n_pages * page

    front = [jnp.zeros((pad, D), F32), meta_tokens.astype(F32)]
    h0 = jnp.concatenate([p for b in range(B) for p in front + [x_prompt[b]]] + [x_sample.reshape(RS, D)], axis=0)

    inv = ROPE_THETA ** (-jnp.arange(half, dtype=F32) / half)

    def rope_tables(pos, reps):
        ang = pos.astype(F32)[:, None] * inv[None, :]
        zl = jnp.zeros((pos.shape[0], LANES - rope), F32)
        return (jnp.tile(jnp.concatenate([jnp.cos(ang), jnp.cos(ang), zl], axis=1), (reps, 1)),
                jnp.tile(jnp.concatenate([jnp.sin(ang), jnp.sin(ang), zl], axis=1), (reps, 1)))

    cs_p, sn_p = rope_tables(jnp.arange(LP, dtype=jnp.int32) - pad, B)
    cs_s, sn_s = rope_tables(past + jnp.arange(T, dtype=jnp.int32), NB)
    cs = jnp.concatenate([cs_p, cs_s], axis=0)
    sn = jnp.concatenate([sn_p, sn_s], axis=0)

    def rot_cols(w):
        return jnp.concatenate([-w[..., half:], w[..., :half]], axis=-1)

    def lane_pad(w):
        return jnp.concatenate([w, jnp.zeros(w.shape[:-1] + (LANES - w.shape[-1],), w.dtype)], axis=-1)

    w_in = mla_w_in[0]
    o_kr = lq + lkv
    w_kr = w_in[:, o_kr:o_kr + rope]
    w_in_ext = jnp.concatenate([w_in[:, :o_kr], w_in[:, o_kr + rope:], lane_pad(w_kr), lane_pad(rot_cols(w_kr))],
                               axis=1).astype(BF16)
    n_in = w_in_ext.shape[1]
    w_uq = mla_w_uq[0]
    w_qn = w_uq[:, :, :nope].reshape(lq, H * nope).astype(BF16)
    w_qa = lane_pad(w_uq[:, :, nope:]).reshape(lq, H * LANES).astype(BF16)
    w_qb = lane_pad(rot_cols(w_uq[:, :, nope:])).reshape(lq, H * LANES).astype(BF16)
    w_uk = mla_w_uk[0].reshape(lkv, H * nope).astype(BF16)
    w_uv = mla_w_uv[0].reshape(lkv, H * vd).astype(BF16)
    w_ukt = jnp.transpose(mla_w_uk[0], (1, 2, 0)).astype(BF16)
    w_out0 = mla_w_out[0].astype(BF16)
    w_sb_in = sb_w_in[0].astype(BF16)
    w_out1 = sb_w_out[0].astype(BF16)
    g0 = mla_norm[0].reshape(1, D)
    gq = mla_q_norm[0].reshape(1, lq)
    gkv = mla_kv_norm[0].reshape(1, lkv)
    g1 = sb_norm[0].reshape(1, D)
    gf = final_norm.reshape(1, D)

    row_spec = lambda w: pl.BlockSpec((tm, w), lambda i: (i, 0))

    cqn, c_all, kr_all, krp_all, gate0 = pl.pallas_call(
        functools.partial(_mla_in_kernel, lq=lq, lkv=lkv, nbr=nbr, rope=rope),
        grid=(nblk,),
        in_specs=[row_spec(D), _const_spec((1, D)), _const_spec((D, n_in)), _const_spec((1, lq)),
                  _const_spec((1, lkv)), row_spec(LANES), row_spec(LANES)],
        out_specs=[row_spec(lq), row_spec(lkv), row_spec(rope), row_spec(LANES), row_spec(nbr)],
        out_shape=[jax.ShapeDtypeStruct((R, lq), BF16), jax.ShapeDtypeStruct((R, lkv), F32),
                   jax.ShapeDtypeStruct((R, rope), F32), jax.ShapeDtypeStruct((R, LANES), BF16),
                   jax.ShapeDtypeStruct((R, nbr), F32)],
        compiler_params=_cparams("arbitrary"), name="mla_in",
    )(h0, g0, w_in_ext, gq, gkv, cs, sn)

    qn_all, qr_all, kn_all, v_all = pl.pallas_call(
        functools.partial(_mla_up_kernel, heads=H),
        grid=(nblk,),
        in_specs=[row_spec(lq), row_spec(lkv), row_spec(LANES), row_spec(LANES),
                  _const_spec((lq, H * nope)), _const_spec((lq, H * LANES)), _const_spec((lq, H * LANES)),
                  _const_spec((lkv, H * nope)), _const_spec((lkv, H * vd))],
        out_specs=[row_spec(H * nope), row_spec(H * LANES), row_spec(H * nope), row_spec(H * vd)],
        out_shape=[jax.ShapeDtypeStruct((R, H * nope), BF16), jax.ShapeDtypeStruct((R, H * LANES), BF16),
                   jax.ShapeDtypeStruct((R, H * nope), BF16), jax.ShapeDtypeStruct((R, H * vd), BF16)],
        compiler_params=_cparams("arbitrary"), name="mla_up",
    )(cqn, c_all, cs, sn, w_qn, w_qa, w_qb, w_uk, w_uv)

    chunks = _query_chunks(LP, QUERY_CHUNK)
    mla_scale = float(nope + rope) ** -0.5
    head_spec = pl.BlockSpec((LP, LANES), lambda b, h: (b, h))
    ogp0 = pl.pallas_call(
        functools.partial(_mla_prompt_attn_kernel, chunks=chunks, pad=pad, scale=mla_scale),
        grid=(B, H),
        in_specs=[head_spec, head_spec, head_spec, pl.BlockSpec((LP, LANES), lambda b, h: (b, 0)),
                  head_spec, head_spec],
        out_specs=head_spec,
        out_shape=jax.ShapeDtypeStruct((RP, nbr), BF16),
        compiler_params=_cparams("arbitrary", "arbitrary"), name="mla_prompt_attn",
    )(qn_all, qr_all, kn_all, krp_all, v_all, gate0)

    sblk = RP // RS
    qlat = pl.pallas_call(
        _mla_qlat_kernel,
        grid=(H,),
        in_specs=[pl.BlockSpec((RS, nope), lambda h: (sblk, h)),
                  pl.BlockSpec((None, nope, lkv), lambda h: (h, 0, 0))],
        out_specs=pl.BlockSpec((None, RS, lkv), lambda h: (h, 0, 0)),
        out_shape=jax.ShapeDtypeStruct((H, RS, lkv), BF16),
        compiler_params=_cparams("arbitrary"), name="mla_qlat",
    )(qn_all, w_ukt)
    qlat_s = jnp.transpose(qlat.reshape(H, NB, T, lkv), (1, 2, 0, 3)).reshape(NB, T * H, lkv)
    qrope_s = qr_all[RP:].reshape(NB, T, H, LANES)[..., :rope].reshape(NB, T * H, rope)
    tp = 8
    cnew = jnp.pad(c_all[RP:].reshape(NB, T, lkv), ((0, 0), (0, tp - T), (0, 0)))
    krnew_t = jnp.pad(jnp.swapaxes(kr_all[RP:].reshape(NB, T, rope), 1, 2), ((0, 0), (0, 0), (0, page - T)))
    krope_t = jnp.swapaxes(cache_mla_krope, 2, 3)

    nchunk = n_pages // npp
    nrow = T * H
    seq_spec = lambda r, w: pl.BlockSpec((None, r, w), lambda s, pt: (s, 0, 0))
    any_spec = pl.BlockSpec(memory_space=pl.ANY)
    olat = pl.pallas_call(
        functools.partial(_mla_sample_attn_kernel, npp=npp, nchunk=nchunk, t_new=T, heads=H, scale=mla_scale),
        grid_spec=pltpu.PrefetchScalarGridSpec(
            num_scalar_prefetch=1, grid=(NB,),
            in_specs=[seq_spec(nrow, lkv), seq_spec(nrow, rope), seq_spec(tp, lkv), seq_spec(rope, page),
                      any_spec, any_spec],
            out_specs=seq_spec(nrow, lkv),
            scratch_shapes=[pltpu.VMEM((RING_SLOTS, npp, page, lkv), F32),
                            pltpu.VMEM((RING_SLOTS, npp, rope, page), F32),
                            pltpu.SemaphoreType.DMA((2, RING_SLOTS)),
                            pltpu.VMEM((SOFTMAX_STREAMS, nrow, 1), F32), pltpu.VMEM((SOFTMAX_STREAMS, nrow, 1), F32),
                            pltpu.VMEM((SOFTMAX_STREAMS, nrow, lkv), F32)]),
        out_shape=jax.ShapeDtypeStruct((NB, nrow, lkv), BF16),
        compiler_params=_cparams("arbitrary"), name="mla_sample_attn",
    )(page_table, qlat_s, qrope_s, cnew, krnew_t, cache_mla_latent, krope_t)
    olat_h = jnp.transpose(olat.reshape(NB, T, H, lkv), (2, 0, 1, 3)).reshape(H, RS, lkv)
    ogs0 = pl.pallas_call(
        _mla_sample_out_kernel,
        grid=(H,),
        in_specs=[pl.BlockSpec((None, RS, lkv), lambda h: (h, 0, 0)),
                  pl.BlockSpec((lkv, vd), lambda h: (0, h)),
                  pl.BlockSpec((RS, vd), lambda h: (sblk, h))],
        out_specs=pl.BlockSpec((RS, vd), lambda h: (0, h)),
        out_shape=jax.ShapeDtypeStruct((RS, nbr), BF16),
        compiler_params=_cparams("arbitrary"), name="mla_sample_out",
    )(olat_h, w_uv, gate0)

    npb = RP // tm
    h1 = pl.pallas_call(
        functools.partial(_outproj_kernel, n_prompt_blocks=npb),
        grid=(nblk,),
        in_specs=[row_spec(D),
                  pl.BlockSpec((tm, nbr), lambda i: (jnp.minimum(i, npb - 1), 0)),
                  pl.BlockSpec((tm, nbr), lambda i: (jnp.maximum(i - npb, 0), 0)),
                  _const_spec((nbr, D))],
        out_specs=row_spec(D),
        out_shape=jax.ShapeDtypeStruct((R, D), F32),
        compiler_params=_cparams("arbitrary"), name="outproj0",
    )(h0, ogp0, ogs0, w_out0)

    sbq, sbk, sbv, gate1 = pl.pallas_call(
        functools.partial(_sb_in_kernel, nbr=sb_nbr, kvw=kvw),
        grid=(nblk,),
        in_specs=[row_spec(D), _const_spec((1, D)), _const_spec((D, 2 * sb_nbr + 2 * kvw))],
        out_specs=[row_spec(sb_nbr), row_spec(kvw), row_spec(kvw), row_spec(sb_nbr)],
        out_shape=[jax.ShapeDtypeStruct((R, sb_nbr), BF16), jax.ShapeDtypeStruct((R, kvw), F32),
                   jax.ShapeDtypeStruct((R, kvw), F32), jax.ShapeDtypeStruct((R, sb_nbr), F32)],
        compiler_params=_cparams("arbitrary"), name="sb_in",
    )(h1, g1, w_sb_in)

    kb = SB_KEY_BLOCK
    tri = (lax.broadcasted_iota(jnp.int32, (kb, kb), 0) >= lax.broadcasted_iota(jnp.int32, (kb, kb), 1)).astype(BF16)
    sb_scale2 = float(hd) ** -0.5 * LOG2_E
    kv_spec = pl.BlockSpec((LP, hd), lambda b, h: (b, h // HG))
    ogp1 = pl.pallas_call(
        functools.partial(_sb_prompt_attn_kernel, chunks=_sb_query_chunks(LP, SB_QUERY_CHUNK), pad=pad,
                          scale2=sb_scale2, kb=kb),
        grid=(B, SH),
        in_specs=[head_spec, kv_spec, kv_spec, head_spec, pl.BlockSpec((kb, kb), lambda b, h: (0, 0))],
        out_specs=head_spec,
        scratch_shapes=[pltpu.VMEM((LP, hd), F32), pltpu.VMEM((LP, LANES), F32)],
        out_shape=jax.ShapeDtypeStruct((RP, sb_nbr), BF16),
        compiler_params=_cparams("arbitrary", "arbitrary"), name="sb_prompt_attn",
    )(sbq, sbk, sbv, gate1, tri)

    rpg = T * HG
    q_s = jnp.transpose(sbq[RP:].reshape(NB, T, G, HG, hd), (0, 2, 1, 3, 4)).reshape(NB, G, rpg, hd)
    knew = jnp.pad(sbk[RP:].reshape(NB, T, kvw), ((0, 0), (0, tp - T), (0, 0)))
    vnew = jnp.pad(sbv[RP:].reshape(NB, T, kvw), ((0, 0), (0, tp - T), (0, 0)))
    ck = cache_sb_k.reshape(cache_sb_k.shape[:2] + (page * G, hd))
    cv = cache_sb_v.reshape(cache_sb_v.shape[:2] + (page * G, hd))
    npp = SB_PAGES_PER_CHUNK
    assert n_pages % npp == 0 and (npp * page) % kb == 0
    o_s = pl.pallas_call(
        functools.partial(_sb_sample_attn_kernel, npp=npp, nchunk=n_pages // npp, n_pages=n_pages, groups=G,
                          scale2=sb_scale2, t_new=T, kb=kb),
        grid_spec=pltpu.PrefetchScalarGridSpec(
            num_scalar_prefetch=1, grid=(NB,),
            in_specs=[pl.BlockSpec((None, G, rpg, hd), lambda s, pt: (s, 0, 0, 0)),
                      seq_spec(tp, kvw), seq_spec(tp, kvw),
                      pl.BlockSpec((kb, kb), lambda s, pt: (0, 0)), any_spec, any_spec],
            out_specs=seq_spec(G * rpg, hd),
            scratch_shapes=[pltpu.VMEM((2, npp, page * G, hd), F32), pltpu.VMEM((2, npp, page * G, hd), F32),
                            pltpu.SemaphoreType.DMA((2, 2)),
                            pltpu.VMEM((RING_SLOTS, npp, page * G, hd), F32),
                            pltpu.VMEM((RING_SLOTS, npp, page * G, hd), F32),
                            pltpu.SemaphoreType.DMA((2, RING_SLOTS)),
                            pltpu.VMEM((G * rpg, LANES), F32), pltpu.VMEM((G * rpg, hd), F32)]),
        out_shape=jax.ShapeDtypeStruct((NB, G * rpg, hd), F32),
        compiler_params=_cparams("arbitrary"), name="sb_sample_attn",
    )(page_table, q_s, knew, vnew, tri, ck, cv)
    o_tok = jnp.transpose(o_s.reshape(NB, G, T, HG, hd), (0, 2, 1, 3, 4)).reshape(RS, sb_nbr)
    ogs1 = pl.pallas_call(
        _gate_kernel,
        grid=(1,),
        in_specs=[pl.BlockSpec((RS, sb_nbr), lambda i: (0, 0)), pl.BlockSpec((RS, sb_nbr), lambda i: (sblk, 0))],
        out_specs=pl.BlockSpec((RS, sb_nbr), lambda i: (0, 0)),
        out_shape=jax.ShapeDtypeStruct((RS, sb_nbr), BF16),
        compiler_params=_cparams("arbitrary"), name="sb_sample_gate",
    )(o_tok, gate1)

    bpl = LP // ROW_ALIGN
    spb = S // ROW_ALIGN
    tf = ROW_ALIGN
    y_prompt = pl.pallas_call(
        _outproj_final_kernel,
        grid=(B, spb),
        in_specs=[pl.BlockSpec((tf, D), lambda b, i: (b * bpl + (bpl - spb) + i, 0)),
                  pl.BlockSpec((tf, sb_nbr), lambda b, i: (b * bpl + (bpl - spb) + i, 0)),
                  _const_spec((sb_nbr, D)), _const_spec((1, D))],
        out_specs=pl.BlockSpec((None, tf, D), lambda b, i: (b, i, 0)),
        out_shape=jax.ShapeDtypeStruct((B, S, D), F32),
        compiler_params=_cparams("arbitrary", "arbitrary"), name="outproj1_prompt",
    )(h1, ogp1, w_out1, gf)
    y_sample = pl.pallas_call(
        _outproj_final_kernel,
        grid=(RS // tf,),
        in_specs=[pl.BlockSpec((tf, D), lambda i: (RP // tf + i, 0)),
                  pl.BlockSpec((tf, sb_nbr), lambda i: (i, 0)),
                  _const_spec((sb_nbr, D)), _const_spec((1, D))],
        out_specs=pl.BlockSpec((tf, D), lambda i: (i, 0)),
        out_shape=jax.ShapeDtypeStruct((RS, D), F32),
        compiler_params=_cparams("arbitrary"), name="outproj1_sample",
    )(h1, ogs1, w_out1, gf).reshape(NB, T, D)

    def prompt_rows(a):
        return a[:RP].reshape((B, LP) + a.shape[1:])[:, pad:][None]

    def sample_rows(a):
        return a[RP:].reshape((NB, T) + a.shape[1:])[None]

    sbk4 = sbk.reshape(R, G, hd)
    sbv4 = sbv.reshape(R, G, hd)
    return (y_prompt, y_sample,
            prompt_rows(c_all), prompt_rows(kr_all), prompt_rows(sbk4), prompt_rows(sbv4),
            sample_rows(c_all), sample_rows(kr_all), sample_rows(sbk4), sample_rows(sbv4))
```
